```python
import math
import jax, jax.numpy as jnp
from jax import lax
import numpy as np

D_MODEL = 1024
BATCH = 4
SEQ = 4096
DEPTH = 1

CHUNK = 64
Q_BLOCK = 128

MLA_HEADS = 8
QK_NOPE_DIM = 64
QK_ROPE_DIM = 32
QK_HEAD_DIM = QK_NOPE_DIM + QK_ROPE_DIM
V_HEAD_DIM = 64
MLA_WIDTH = MLA_HEADS * V_HEAD_DIM
Q_LORA_RANK = 256
KV_LORA_RANK = 128
ROPE_THETA = 10000.0

RNN_WIDTH = 512
RNN_BLOCKS = 8
RNN_BLOCK_DIM = RNN_WIDTH // RNN_BLOCKS
CONV_WIDTH = 4
LRU_C = 8.0

MIX_WIDTH = MLA_WIDTH + RNN_WIDTH

IN_SPLITS = (Q_LORA_RANK, KV_LORA_RANK, QK_ROPE_DIM, MLA_WIDTH, RNN_WIDTH, RNN_WIDTH)
IN_WIDTH = sum(IN_SPLITS)

NORM_EPS = 1e-6

kernel_name = "hymba_mla_rglru_chunk_causal"


def rmsnorm(x, g):
    xf = x.astype(jnp.float32)
    y = xf * lax.rsqrt(jnp.mean(xf * xf, axis=-1, keepdims=True) + NORM_EPS)
    return (y * g.astype(jnp.float32)).astype(x.dtype)


def rope_tables(seq):
    pos = jnp.arange(seq, dtype=jnp.float32)
    inv_freq = ROPE_THETA ** (-jnp.arange(0, QK_ROPE_DIM, 2, dtype=jnp.float32) / QK_ROPE_DIM)
    ang = pos[:, None] * inv_freq[None, :]
    return jnp.cos(ang), jnp.sin(ang)


def apply_rope(x, cos, sin):
    xf = x.astype(jnp.float32)
    x1, x2 = jnp.split(xf, 2, axis=-1)
    out = jnp.concatenate([x1 * cos - x2 * sin, x2 * cos + x1 * sin], axis=-1)
    return out.astype(x.dtype)


def chunk_causal_attention(q, k, v):
    b, s, h, dqk = q.shape
    nqb = s // Q_BLOCK
    scale = 1.0 / math.sqrt(QK_HEAD_DIM)
    qb = q.reshape(b, nqb, Q_BLOCK, h, dqk).transpose(1, 0, 2, 3, 4)
    kf = k.astype(jnp.float32)
    key_chunk = jnp.arange(s) // CHUNK

    def one_block(args):
        q_blk, bi = args
        q_pos = bi * Q_BLOCK + jnp.arange(Q_BLOCK)
        sc = jnp.einsum('bqhd,bkhd->bhqk', q_blk.astype(jnp.float32), kf) * scale
        mask = key_chunk[None, :] <= (q_pos // CHUNK)[:, None]
        sc = jnp.where(mask[None, None], sc, jnp.finfo(jnp.float32).min)
        p = jax.nn.softmax(sc, axis=-1)
        return jnp.einsum('bhqk,bkhd->bqhd', p.astype(v.dtype), v)

    out = lax.map(one_block, (qb, jnp.arange(nqb)))
    return out.transpose(1, 0, 2, 3, 4).reshape(b, s, h * v.shape[-1])


def causal_depthwise_conv(x, w, bias):
    c = x.shape[-1]
    y = lax.conv_general_dilated(
        x, w[:, None, :].astype(x.dtype), window_strides=(1,),
        padding=((CONV_WIDTH - 1, 0),), dimension_numbers=('NWC', 'WIO', 'NWC'),
        feature_group_count=c)
    return y + bias.astype(x.dtype)


def rg_lru(x, w_a, b_a, w_x, b_x, lru_param):
    b, s, _ = x.shape
    xf = x.astype(jnp.float32)
    xb = xf.reshape(b, s, RNN_BLOCKS, RNN_BLOCK_DIM)
    r = jax.nn.sigmoid(jnp.einsum('bsnc,ncd->bsnd', xb, w_a.astype(jnp.float32)).reshape(b, s, -1)
                       + b_a.astype(jnp.float32))
    i = jax.nn.sigmoid(jnp.einsum('bsnc,ncd->bsnd', xb, w_x.astype(jnp.float32)).reshape(b, s, -1)
                       + b_x.astype(jnp.float32))
    log_a = -LRU_C * r * jax.nn.softplus(-lru_param.astype(jnp.float32))
    a = jnp.exp(log_a)
    mult = jnp.sqrt(jnp.clip(1.0 - jnp.exp(2.0 * log_a), 1e-12, None))
    is_start = (jnp.arange(s) == 0)[None, :, None]
    mult = jnp.where(is_start, 1.0, mult)
    u = mult * (i * xf)

    def combine(left, right):
        a_l, h_l = left
        a_r, h_r = right
        return a_l * a_r, a_r * h_l + h_r

    _, h = lax.associative_scan(combine, (a, u), axis=1)
    return h.astype(x.dtype)


def setup_inputs(seed: int = 0) -> dict:
    key = jax.random.key(seed)
    ks = jax.random.split(key, 20)
    f32 = jnp.float32

    def w(k, shape, fan_in):
        return jax.random.normal(k, shape, f32) * (fan_in ** -0.5)

    def gain(k, shape):
        return 1.0 + 0.05 * jax.random.normal(k, shape, f32)

    def bias(k, shape):
        return 0.01 * jax.random.normal(k, shape, f32)

    u = jax.random.uniform(ks[13], (DEPTH, RNN_WIDTH), f32, 0.9, 0.999)
    base = u ** (1.0 / LRU_C)
    lru_param = jnp.log(base) - jnp.log1p(-base)

    return {
        "x": jax.random.normal(ks[0], (BATCH, SEQ, D_MODEL), f32),
        "norm_in_g": gain(ks[1], (DEPTH, D_MODEL)),
        "w_in": w(ks[2], (DEPTH, D_MODEL, IN_WIDTH), D_MODEL),
        "q_norm_g": gain(ks[3], (DEPTH, Q_LORA_RANK)),
        "w_q_b": w(ks[4], (DEPTH, Q_LORA_RANK, MLA_HEADS * QK_HEAD_DIM), Q_LORA_RANK),
        "kv_norm_g": gain(ks[5], (DEPTH, KV_LORA_RANK)),
        "w_kv_b": w(ks[6], (DEPTH, KV_LORA_RANK, MLA_HEADS * (QK_NOPE_DIM + V_HEAD_DIM)), KV_LORA_RANK),
        "conv_w": w(ks[7], (DEPTH, CONV_WIDTH, RNN_WIDTH), CONV_WIDTH),
        "conv_b": bias(ks[8], (DEPTH, RNN_WIDTH)),
        "w_rg_a": w(ks[9], (DEPTH, RNN_BLOCKS, RNN_BLOCK_DIM, RNN_BLOCK_DIM), RNN_BLOCK_DIM),
        "b_rg_a": bias(ks[10], (DEPTH, RNN_WIDTH)),
        "w_rg_x": w(ks[11], (DEPTH, RNN_BLOCKS, RNN_BLOCK_DIM, RNN_BLOCK_DIM), RNN_BLOCK_DIM),
        "b_rg_x": bias(ks[12], (DEPTH, RNN_WIDTH)),
        "lru_param": lru_param,
        "out_norm_mla_g": gain(ks[14], (DEPTH, MLA_WIDTH)),
        "out_norm_rnn_g": gain(ks[15], (DEPTH, RNN_WIDTH)),
        "w_out": w(ks[16], (DEPTH, MIX_WIDTH, D_MODEL), MIX_WIDTH),
        "final_norm_g": gain(ks[17], (D_MODEL,)),
    }


def reference(x, norm_in_g, w_in, q_norm_g, w_q_b, kv_norm_g, w_kv_b, conv_w, conv_b,
              w_rg_a, b_rg_a, w_rg_x, b_rg_x, lru_param, out_norm_mla_g, out_norm_rnn_g,
              w_out, final_norm_g):
    b, s, _ = x.shape
    cos, sin = rope_tables(s)
    offs = np.cumsum(IN_SPLITS)[:-1].tolist()

    for l in range(DEPTH):
        h = rmsnorm(x, norm_in_g[l])
        z = jnp.einsum('bsd,de->bse', h, w_in[l])
        q_lat, kv_lat, k_rope, g_mla, x_rnn, g_rnn = jnp.split(z, offs, axis=-1)

        q = jnp.einsum('bsr,re->bse', rmsnorm(q_lat, q_norm_g[l]), w_q_b[l])
        q = q.reshape(b, s, MLA_HEADS, QK_HEAD_DIM)
        q_nope, q_pe = q[..., :QK_NOPE_DIM], q[..., QK_NOPE_DIM:]
        q_pe = apply_rope(q_pe, cos[:, None, :], sin[:, None, :])
        kv = jnp.einsum('bsr,re->bse', rmsnorm(kv_lat, kv_norm_g[l]), w_kv_b[l])
        kv = kv.reshape(b, s, MLA_HEADS, QK_NOPE_DIM + V_HEAD_DIM)
        k_nope, v = kv[..., :QK_NOPE_DIM], kv[..., QK_NOPE_DIM:]
        k_pe = apply_rope(k_rope, cos, sin)
        k_pe = jnp.broadcast_to(k_pe[:, :, None, :], (b, s, MLA_HEADS, QK_ROPE_DIM))
        q_full = jnp.concatenate([q_nope, q_pe], axis=-1)
        k_full = jnp.concatenate([k_nope, k_pe], axis=-1)
        y_mla = chunk_causal_attention(q_full, k_full, v)

        xr = causal_depthwise_conv(x_rnn, conv_w[l], conv_b[l])
        y_rnn = rg_lru(xr, w_rg_a[l], b_rg_a[l], w_rg_x[l], b_rg_x[l], lru_param[l])

        y_mla = rmsnorm(y_mla, out_norm_mla_g[l]) * jax.nn.silu(g_mla)
        y_rnn = rmsnorm(y_rnn, out_norm_rnn_g[l]) * jax.nn.silu(g_rnn)
        y = jnp.concatenate([y_mla, y_rnn], axis=-1)
        x = x + jnp.einsum('bse,ed->bsd', y, w_out[l])

    return rmsnorm(x, final_norm_g)
```

```python
import functools
import math

import jax
import jax.numpy as jnp
from jax import lax
from jax.experimental import pallas as pl
from jax.experimental.pallas import tpu as pltpu

D_MODEL = 1024
CHUNK = 64
MLA_HEADS = 8
QK_NOPE_DIM = 64
QK_ROPE_DIM = 32
QK_HEAD_DIM = QK_NOPE_DIM + QK_ROPE_DIM
V_HEAD_DIM = 64
MLA_WIDTH = MLA_HEADS * V_HEAD_DIM
Q_LORA_RANK = 256
KV_LORA_RANK = 128
ROPE_THETA = 10000.0
RNN_WIDTH = 512
RNN_BLOCKS = 8
RNN_BLOCK_DIM = RNN_WIDTH // RNN_BLOCKS
CONV_WIDTH = 4
LRU_C = 8.0
NORM_EPS = 1e-6

LANES = 128
SUBLANES = 8
HEAD_PAD = LANES
Z_WIDTH = 2048
SEQ_TILE = 512
ATT_BLOCK = 256
VMEM_LIMIT = 56 * 1024 * 1024

Z_Q, Z_KV, Z_KPE, Z_GM, Z_XR, Z_GR = 0, 256, 384, 512, 1024, 1536

MASK_VALUE = -0.5 * float(jnp.finfo(jnp.float32).max)

f32 = jnp.float32
bf16 = jnp.bfloat16


def _rms(x, g):
    ms = jnp.mean(x * x, axis=-1, keepdims=True)
    return x * lax.rsqrt(ms + NORM_EPS) * g


def _rope(x, c, s1, s2):
    return x * c + pltpu.roll(x, LANES - 16, 1) * s1 + pltpu.roll(x, 16, 1) * s2


def _proj_kernel(x_ref, gin_ref, win_ref, qg_ref, wq_ref, kvg_ref, wkv_ref,
                 rc_ref, rs1_ref, rs2_ref, convw_ref, convb_ref, wgate_ref,
                 bgate_ref, lru_ref, grnn_ref,
                 q_out, k_out, vt_out, gm_out, yr_out,
                 xr_buf, h_carry):
    s = pl.program_id(1)
    ts = x_ref.shape[1]
    x = x_ref[0]
    h = _rms(x, gin_ref[...]).astype(bf16)
    z = jnp.dot(h, win_ref[...], preferred_element_type=f32)

    c = rc_ref[...]
    s1 = rs1_ref[...]
    s2 = rs2_ref[...]
    scale = 1.0 / math.sqrt(QK_HEAD_DIM)

    qn = _rms(z[:, Z_Q:Z_Q + Q_LORA_RANK], qg_ref[...]).astype(bf16)
    q = jnp.dot(qn, wq_ref[...], preferred_element_type=f32)
    for hh in range(MLA_HEADS):
        qh = _rope(q[:, hh * HEAD_PAD:(hh + 1) * HEAD_PAD], c, s1, s2)
        q_out[0, hh] = (qh * scale).astype(bf16)

    kvn = _rms(z[:, Z_KV:Z_KV + KV_LORA_RANK], kvg_ref[...]).astype(bf16)
    kv = jnp.dot(kvn, wkv_ref[...], preferred_element_type=f32)
    kpe = _rope(z[:, Z_KPE:Z_KPE + HEAD_PAD], c, s1, s2)
    for hh in range(MLA_HEADS):
        k_out[0, hh] = (kv[:, hh * HEAD_PAD:(hh + 1) * HEAD_PAD] + kpe).astype(bf16)
    vt = kv[:, MLA_HEADS * HEAD_PAD:].T
    for j in range(ts // ATT_BLOCK):
        blk = vt[:, j * ATT_BLOCK:(j + 1) * ATT_BLOCK]
        vt_out[0, j] = blk.reshape(MLA_HEADS, V_HEAD_DIM, ATT_BLOCK).astype(bf16)

    gmla = z[:, Z_GM:Z_GM + MLA_WIDTH]
    gm_out[0] = (gmla * jax.nn.sigmoid(gmla)).astype(bf16)

    @pl.when(s == 0)
    def _():
        xr_buf[0:SUBLANES, :] = jnp.zeros((SUBLANES, RNN_WIDTH), f32)
        h_carry[...] = jnp.zeros((1, RNN_WIDTH), f32)

    xr_buf[SUBLANES:SUBLANES + ts, :] = z[:, Z_XR:Z_XR + RNN_WIDTH]
    xr = convb_ref[...]
    for k in range(CONV_WIDTH):
        off = SUBLANES - (CONV_WIDTH - 1) + k
        xr = xr + convw_ref[k:k + 1, :] * xr_buf[pl.ds(off, ts), :]
    xr_buf[0:SUBLANES, :] = xr_buf[ts:ts + SUBLANES, :]

    gates = jnp.dot(xr.astype(bf16), wgate_ref[...], preferred_element_type=f32)
    gates = gates + bgate_ref[...]
    r = jax.nn.sigmoid(gates[:, :RNN_WIDTH])
    i = jax.nn.sigmoid(gates[:, RNN_WIDTH:])
    nl = -lru_ref[...]
    softplus = jnp.maximum(nl, 0.0) + jnp.log1p(jnp.exp(-jnp.abs(nl)))
    log_a = (-LRU_C) * r * softplus
    a = jnp.exp(log_a)
    mult = jnp.sqrt(jnp.maximum(1.0 - a * a, 1e-12))
    row = lax.broadcasted_iota(jnp.int32, (ts, RNN_WIDTH), 0)
    mult = jnp.where((row == 0) & (s == 0), 1.0, mult)
    u = mult * (i * xr)

    groups = ts // SUBLANES
    a3 = a.reshape(groups, SUBLANES, RNN_WIDTH)
    u3 = u.reshape(groups, SUBLANES, RNN_WIDTH)
    sub = lax.broadcasted_iota(jnp.int32, (1, SUBLANES, RNN_WIDTH), 1)
    for d in (1, 2, 4):
        keep = sub >= d
        u3 = jnp.where(keep, a3 * pltpu.roll(u3, d, 1) + u3, u3)
        a3 = jnp.where(keep, a3 * pltpu.roll(a3, d, 1), a3)
    carry = h_carry[...]
    outs = []
    for g in range(groups):
        hg = u3[g] + a3[g] * carry
        outs.append(hg)
        carry = hg[SUBLANES - 1:SUBLANES, :]
    h_carry[...] = carry
    y = jnp.concatenate(outs, axis=0)

    gr = z[:, Z_GR:Z_GR + RNN_WIDTH]
    yr_out[0] = (_rms(y, grnn_ref[...]) * (gr * jax.nn.sigmoid(gr))).astype(bf16)


def _attn_kernel(q_ref, k_ref, vt_ref, o_ref, m_ref, l_ref, acc_ref):
    qi = pl.program_id(1)
    tq = q_ref.shape[2]

    m_ref[...] = jnp.full(m_ref.shape, MASK_VALUE, f32)
    l_ref[...] = jnp.zeros(l_ref.shape, f32)
    acc_ref[...] = jnp.zeros(acc_ref.shape, f32)

    def block(j, masked):
        for hh in range(MLA_HEADS):
            kb = k_ref[0, hh, pl.ds(pl.multiple_of(j * tq, tq), tq), :]
            st = lax.dot_general(kb, q_ref[0, hh], (((1,), (1,)), ((), ())),
                                 preferred_element_type=f32)
            if masked:
                kc = lax.broadcasted_iota(jnp.int32, st.shape, 0) // CHUNK
                qc = lax.broadcasted_iota(jnp.int32, st.shape, 1) // CHUNK
                st = jnp.where(kc <= qc, st, MASK_VALUE)
            m_prev = m_ref[hh]
            m_new = jnp.maximum(m_prev, jnp.max(st, axis=0, keepdims=True))
            alpha = jnp.exp(m_prev - m_new)
            p = jnp.exp(st - m_new)
            l_ref[hh] = alpha * l_ref[hh] + jnp.sum(p, axis=0, keepdims=True)
            pv = jnp.dot(vt_ref[0, j, hh], p.astype(bf16), preferred_element_type=f32)
            acc_ref[hh] = alpha * acc_ref[hh] + pv
            m_ref[hh] = m_new

    def body(j, carry):
        block(j, False)
        return carry

    lax.fori_loop(0, qi, body, 0)
    block(qi, True)

    for hh in range(MLA_HEADS):
        o_ref[0, hh * V_HEAD_DIM:(hh + 1) * V_HEAD_DIM, :] = acc_ref[hh] / l_ref[hh]


def _out_kernel(yt_ref, gm_ref, yr_ref, x_ref, gmla_ref, wout_ref, gfin_ref, o_ref):
    ym = yt_ref[0].T
    ym = _rms(ym, gmla_ref[...]) * gm_ref[0].astype(f32)
    y = jnp.concatenate([ym.astype(bf16), yr_ref[0]], axis=-1)
    xo = x_ref[0] + jnp.dot(y, wout_ref[...], preferred_element_type=f32)
    o_ref[0] = _rms(xo, gfin_ref[...])


def _prepare_weights(w_in, w_q_b, w_kv_b, w_rg_a, w_rg_x):
    offs = [0, 256, 384, 416, 928, 1440, 1952]
    w_qlat, w_kvlat, w_krope, w_gm, w_xr, w_gr = (
        w_in[:, offs[i]:offs[i + 1]] for i in range(6))
    kpe_blk = jnp.zeros((D_MODEL, HEAD_PAD), w_in.dtype)
    kpe_blk = kpe_blk.at[:, QK_NOPE_DIM:QK_HEAD_DIM].set(w_krope)
    w_in_r = jnp.concatenate([w_qlat, w_kvlat, kpe_blk, w_gm, w_xr, w_gr], axis=1)

    wq = w_q_b.reshape(Q_LORA_RANK, MLA_HEADS, QK_HEAD_DIM)
    wq = jnp.pad(wq, ((0, 0), (0, 0), (0, HEAD_PAD - QK_HEAD_DIM)))
    wq = wq.reshape(Q_LORA_RANK, MLA_HEADS * HEAD_PAD)

    wkv = w_kv_b.reshape(KV_LORA_RANK, MLA_HEADS, QK_NOPE_DIM + V_HEAD_DIM)
    wk = jnp.pad(wkv[:, :, :QK_NOPE_DIM], ((0, 0), (0, 0), (0, HEAD_PAD - QK_NOPE_DIM)))
    wk = wk.reshape(KV_LORA_RANK, MLA_HEADS * HEAD_PAD)
    wv = wkv[:, :, QK_NOPE_DIM:].reshape(KV_LORA_RANK, MLA_WIDTH)
    wkv_r = jnp.concatenate([wk, wv], axis=1)

    eye = jnp.eye(RNN_BLOCKS, dtype=w_rg_a.dtype)

    def block_diag(w):
        return jnp.einsum('ncd,nm->ncmd', w, eye).reshape(RNN_WIDTH, RNN_WIDTH)

    wgate = jnp.concatenate([block_diag(w_rg_a), block_diag(w_rg_x)], axis=1)
    return (w_in_r.astype(bf16), wq.astype(bf16), wkv_r.astype(bf16), wgate.astype(bf16))


def _rope_tables(seq):
    pos = jnp.arange(seq, dtype=f32)
    inv_freq = ROPE_THETA ** (-jnp.arange(0, QK_ROPE_DIM, 2, dtype=f32) / QK_ROPE_DIM)
    ang = pos[:, None] * inv_freq[None, :]
    cos, sin = jnp.cos(ang), jnp.sin(ang)
    half = QK_ROPE_DIM // 2
    zeros = jnp.zeros((seq, half), f32)
    pad_l = jnp.zeros((seq, QK_NOPE_DIM), f32)
    pad_r = jnp.zeros((seq, HEAD_PAD - QK_HEAD_DIM), f32)
    c = jnp.concatenate([jnp.ones((seq, QK_NOPE_DIM), f32), cos, cos, pad_r], axis=1)
    s1 = jnp.concatenate([pad_l, -sin, zeros, pad_r], axis=1)
    s2 = jnp.concatenate([pad_l, zeros, sin, pad_r], axis=1)
    return c, s1, s2


def _const_spec(shape):
    return pl.BlockSpec(shape, lambda *_: (0,) * len(shape))


def kernel(x, norm_in_g, w_in, q_norm_g, w_q_b, kv_norm_g, w_kv_b, conv_w, conv_b,
           w_rg_a, b_rg_a, w_rg_x, b_rg_x, lru_param, out_norm_mla_g, out_norm_rnn_g,
           w_out, final_norm_g):
    b, s, _ = x.shape
    depth = w_in.shape[0]
    ts = SEQ_TILE
    tq = ATT_BLOCK
    assert s % ts == 0 and ts % tq == 0 and tq % CHUNK == 0
    rc, rs1, rs2 = _rope_tables(s)
    row = lambda v: v.reshape(1, -1).astype(f32)
    params = pltpu.CompilerParams(
        dimension_semantics=("arbitrary", "arbitrary"), vmem_limit_bytes=VMEM_LIMIT)

    assert depth == 1, "the final norm is fused into the single layer's output kernel"
    for l in range(depth):
        w_in_r, wq, wkv, wgate = _prepare_weights(
            w_in[l], w_q_b[l], w_kv_b[l], w_rg_a[l], w_rg_x[l])
        bgate = jnp.concatenate([b_rg_a[l], b_rg_x[l]]).reshape(1, -1)

        seq_spec = lambda w: pl.BlockSpec((1, ts, w), lambda bi, si: (bi, si, 0))
        tab_spec = pl.BlockSpec((ts, HEAD_PAD), lambda bi, si: (si, 0))
        head_spec = pl.BlockSpec((1, MLA_HEADS, ts, HEAD_PAD), lambda bi, si: (bi, 0, si, 0))
        q, k, vt, gm, yr = pl.pallas_call(
            _proj_kernel,
            grid=(b, s // ts),
            in_specs=[
                seq_spec(D_MODEL), _const_spec((1, D_MODEL)), _const_spec((D_MODEL, Z_WIDTH)),
                _const_spec((1, Q_LORA_RANK)), _const_spec((Q_LORA_RANK, MLA_HEADS * HEAD_PAD)),
                _const_spec((1, KV_LORA_RANK)),
                _const_spec((KV_LORA_RANK, MLA_HEADS * HEAD_PAD + MLA_WIDTH)),
                tab_spec, tab_spec, tab_spec,
                _const_spec((CONV_WIDTH, RNN_WIDTH)), _const_spec((1, RNN_WIDTH)),
                _const_spec((RNN_WIDTH, 2 * RNN_WIDTH)), _const_spec((1, 2 * RNN_WIDTH)),
                _const_spec((1, RNN_WIDTH)), _const_spec((1, RNN_WIDTH)),
            ],
            out_specs=[
                head_spec, head_spec,
                pl.BlockSpec((1, ts // tq, MLA_HEADS, V_HEAD_DIM, tq),
                             lambda bi, si: (bi, si, 0, 0, 0)),
                seq_spec(MLA_WIDTH), seq_spec(RNN_WIDTH),
            ],
            out_shape=[
                jax.ShapeDtypeStruct((b, MLA_HEADS, s, HEAD_PAD), bf16),
                jax.ShapeDtypeStruct((b, MLA_HEADS, s, HEAD_PAD), bf16),
                jax.ShapeDtypeStruct((b, s // tq, MLA_HEADS, V_HEAD_DIM, tq), bf16),
                jax.ShapeDtypeStruct((b, s, MLA_WIDTH), bf16),
                jax.ShapeDtypeStruct((b, s, RNN_WIDTH), bf16),
            ],
            scratch_shapes=[pltpu.VMEM((ts + 2 * SUBLANES, RNN_WIDTH), f32),
                            pltpu.VMEM((1, RNN_WIDTH), f32)],
            compiler_params=params,
            name="proj_rnn",
        )(x, row(norm_in_g[l]), w_in_r, row(q_norm_g[l]), wq, row(kv_norm_g[l]), wkv,
          rc, rs1, rs2, conv_w[l].astype(f32), row(conv_b[l]), wgate, bgate.astype(f32),
          row(lru_param[l]), row(out_norm_rnn_g[l]))

        yt = pl.pallas_call(
            _attn_kernel,
            grid=(b, s // tq),
            in_specs=[
                pl.BlockSpec((1, MLA_HEADS, tq, HEAD_PAD), lambda bi, qi: (bi, 0, qi, 0)),
                pl.BlockSpec((1, MLA_HEADS, s, HEAD_PAD), lambda bi, qi: (bi, 0, 0, 0)),
                pl.BlockSpec((1, s // tq, MLA_HEADS, V_HEAD_DIM, tq),
                             lambda bi, qi: (bi, 0, 0, 0, 0)),
            ],
            out_specs=pl.BlockSpec((1, MLA_WIDTH, tq), lambda bi, qi: (bi, 0, qi)),
            out_shape=jax.ShapeDtypeStruct((b, MLA_WIDTH, s), f32),
            scratch_shapes=[pltpu.VMEM((MLA_HEADS, 1, tq), f32),
                            pltpu.VMEM((MLA_HEADS, 1, tq), f32),
                            pltpu.VMEM((MLA_HEADS, V_HEAD_DIM, tq), f32)],
            compiler_params=params,
            name="mla_attention",
        )(q, k, vt)

        x = pl.pallas_call(
            _out_kernel,
            grid=(b, s // ts),
            in_specs=[
                pl.BlockSpec((1, MLA_WIDTH, ts), lambda bi, si: (bi, 0, si)),
                seq_spec(MLA_WIDTH), seq_spec(RNN_WIDTH), seq_spec(D_MODEL),
                _const_spec((1, MLA_WIDTH)), _const_spec((D_MODEL, D_MODEL)),
                _const_spec((1, D_MODEL)),
            ],
            out_specs=seq_spec(D_MODEL),
            out_shape=jax.ShapeDtypeStruct((b, s, D_MODEL), f32),
            compiler_params=params,
            name="out_proj",
        )(yt, gm, yr, x, row(out_norm_mla_g[l]), w_out[l].astype(bf16),
          row(final_norm_g))
    return x
```

```python
import functools
import math

import jax
import jax.numpy as jnp
from jax import lax
from jax.experimental import pallas as pl
from jax.experimental.pallas import tpu as pltpu

D_MODEL = 1024
CHUNK = 64
MLA_HEADS = 8
QK_NOPE_DIM = 64
QK_ROPE_DIM = 32
QK_HEAD_DIM = QK_NOPE_DIM + QK_ROPE_DIM
V_HEAD_DIM = 64
MLA_WIDTH = MLA_HEADS * V_HEAD_DIM
Q_LORA_RANK = 256
KV_LORA_RANK = 128
ROPE_THETA = 10000.0
RNN_WIDTH = 512
RNN_BLOCKS = 8
RNN_BLOCK_DIM = RNN_WIDTH // RNN_BLOCKS
CONV_WIDTH = 4
LRU_C = 8.0
NORM_EPS = 1e-6

LANES = 128
SUBLANES = 8
HEAD_PAD = LANES
Z_WIDTH = 2048
SEQ_TILE = 512
ATT_BLOCK = 256
VMEM_LIMIT = 56 * 1024 * 1024

Z_Q, Z_KV, Z_KPE, Z_GM, Z_XR, Z_GR = 0, 256, 384, 512, 1024, 1536

MASK_VALUE = -0.5 * float(jnp.finfo(jnp.float32).max)

f32 = jnp.float32
bf16 = jnp.bfloat16


def _rms(x, g):
    ms = jnp.mean(x * x, axis=-1, keepdims=True)
    return x * lax.rsqrt(ms + NORM_EPS) * g


def _rope(x, c, s1, s2):
    return x * c + pltpu.roll(x, LANES - 16, 1) * s1 + pltpu.roll(x, 16, 1) * s2


def _proj_kernel(x_ref, gin_ref, win_ref, qg_ref, wq_ref, kvg_ref, wkv_ref,
                 rc_ref, rs1_ref, rs2_ref, convw_ref, convb_ref, wgate_ref,
                 bgate_ref, lru_ref, grnn_ref,
                 qt_out, k_out, vt_out, gm_out, yr_out,
                 xr_buf, h_carry):
    s = pl.program_id(1)
    ts = x_ref.shape[1]
    x = x_ref[0]
    h = _rms(x, gin_ref[...]).astype(bf16)
    z = jnp.dot(h, win_ref[...], preferred_element_type=f32)

    c = rc_ref[...]
    s1 = rs1_ref[...]
    s2 = rs2_ref[...]
    scale = math.log2(math.e) / math.sqrt(QK_HEAD_DIM)

    qn = _rms(z[:, Z_Q:Z_Q + Q_LORA_RANK], qg_ref[...]).astype(bf16)
    q = jnp.dot(qn, wq_ref[...], preferred_element_type=f32)
    for hh in range(MLA_HEADS):
        qh = _rope(q[:, hh * HEAD_PAD:(hh + 1) * HEAD_PAD], c, s1, s2) * scale
        qht = qh.T
        for j in range(ts // ATT_BLOCK):
            qt_out[0, j, hh] = qht[:, j * ATT_BLOCK:(j + 1) * ATT_BLOCK].astype(bf16)

    kvn = _rms(z[:, Z_KV:Z_KV + KV_LORA_RANK], kvg_ref[...]).astype(bf16)
    kv = jnp.dot(kvn, wkv_ref[...], preferred_element_type=f32)
    kpe = _rope(z[:, Z_KPE:Z_KPE + HEAD_PAD], c, s1, s2)
    for hh in range(MLA_HEADS):
        k_out[0, hh] = (kv[:, hh * HEAD_PAD:(hh + 1) * HEAD_PAD] + kpe).astype(bf16)
    vt = kv[:, MLA_HEADS * HEAD_PAD:].T
    for j in range(ts // ATT_BLOCK):
        blk = vt[:, j * ATT_BLOCK:(j + 1) * ATT_BLOCK]
        vt_out[0, j] = blk.reshape(MLA_HEADS, V_HEAD_DIM, ATT_BLOCK).astype(bf16)

    gmla = z[:, Z_GM:Z_GM + MLA_WIDTH]
    gm_out[0] = (gmla * jax.nn.sigmoid(gmla)).astype(bf16)

    @pl.when(s == 0)
    def _():
        xr_buf[0:SUBLANES, :] = jnp.zeros((SUBLANES, RNN_WIDTH), f32)
        h_carry[...] = jnp.zeros((1, RNN_WIDTH), f32)

    xr_buf[SUBLANES:SUBLANES + ts, :] = z[:, Z_XR:Z_XR + RNN_WIDTH]
    xr = convb_ref[...]
    for k in range(CONV_WIDTH):
        off = SUBLANES - (CONV_WIDTH - 1) + k
        xr = xr + convw_ref[k:k + 1, :] * xr_buf[pl.ds(off, ts), :]
    xr_buf[0:SUBLANES, :] = xr_buf[ts:ts + SUBLANES, :]

    gates = jnp.dot(xr.astype(bf16), wgate_ref[...], preferred_element_type=f32)
    gates = gates + bgate_ref[...]
    r = jax.nn.sigmoid(gates[:, :RNN_WIDTH])
    i = jax.nn.sigmoid(gates[:, RNN_WIDTH:])
    nl = -lru_ref[...]
    softplus = jnp.maximum(nl, 0.0) + jnp.log1p(jnp.exp(-jnp.abs(nl)))
    log_a = (-LRU_C) * r * softplus
    a = jnp.exp(log_a)
    mult = jnp.sqrt(jnp.maximum(1.0 - a * a, 1e-12))
    row = lax.broadcasted_iota(jnp.int32, (ts, RNN_WIDTH), 0)
    mult = jnp.where((row == 0) & (s == 0), 1.0, mult)
    u = mult * (i * xr)

    groups = ts // SUBLANES
    a3 = a.reshape(groups, SUBLANES, RNN_WIDTH)
    u3 = u.reshape(groups, SUBLANES, RNN_WIDTH)
    sub = lax.broadcasted_iota(jnp.int32, (1, SUBLANES, RNN_WIDTH), 1)
    for d in (1, 2, 4):
        keep = sub >= d
        u3 = jnp.where(keep, a3 * pltpu.roll(u3, d, 1) + u3, u3)
        a3 = jnp.where(keep, a3 * pltpu.roll(a3, d, 1), a3)
    carry = h_carry[...]
    outs = []
    for g in range(groups):
        hg = u3[g] + a3[g] * carry
        outs.append(hg)
        carry = hg[SUBLANES - 1:SUBLANES, :]
    h_carry[...] = carry
    y = jnp.concatenate(outs, axis=0)

    gr = z[:, Z_GR:Z_GR + RNN_WIDTH]
    yr_out[0] = (_rms(y, grnn_ref[...]) * (gr * jax.nn.sigmoid(gr))).astype(bf16)


def _attn_kernel(qt_ref, k_ref, vt_ref, o_ref, s_even, s_odd, m_ref, l_ref, acc_ref):
    qi = pl.program_id(1)
    tq = ATT_BLOCK

    m_ref[...] = jnp.full(m_ref.shape, MASK_VALUE, f32)
    l_ref[...] = jnp.zeros(l_ref.shape, f32)
    acc_ref[...] = jnp.zeros(acc_ref.shape, f32)

    def scores(j, s_dst):
        for hh in range(MLA_HEADS):
            kb = k_ref[0, hh, pl.ds(pl.multiple_of(j * tq, tq), tq), :]
            s_dst[hh] = jnp.dot(kb, qt_ref[0, 0, hh], preferred_element_type=f32)

    def update(j, s_src, masked):
        for hh in range(MLA_HEADS):
            st = s_src[hh]
            if masked:
                kc = lax.broadcasted_iota(jnp.int32, st.shape, 0) // CHUNK
                qc = lax.broadcasted_iota(jnp.int32, st.shape, 1) // CHUNK
                st = jnp.where(kc <= qc, st, MASK_VALUE)
            m_prev = m_ref[hh]
            m_new = jnp.maximum(m_prev, jnp.max(st, axis=0, keepdims=True))
            alpha = jnp.exp2(m_prev - m_new)
            p = jnp.exp2(st - m_new)
            l_ref[hh] = alpha * l_ref[hh] + jnp.sum(p, axis=0, keepdims=True)
            pv = jnp.dot(vt_ref[0, j, hh], p.astype(bf16), preferred_element_type=f32)
            acc_ref[hh] = alpha * acc_ref[hh] + pv
            m_ref[hh] = m_new

    scores(0, s_even)

    def body(j, carry):
        @pl.when(lax.rem(j, 2) == 0)
        def _():
            scores(j + 1, s_odd)
            update(j, s_even, False)

        @pl.when(lax.rem(j, 2) == 1)
        def _():
            scores(j + 1, s_even)
            update(j, s_odd, False)
        return carry

    lax.fori_loop(0, qi, body, 0)

    @pl.when(lax.rem(qi, 2) == 0)
    def _():
        update(qi, s_even, True)

    @pl.when(lax.rem(qi, 2) == 1)
    def _():
        update(qi, s_odd, True)

    for hh in range(MLA_HEADS):
        o_ref[0, hh * V_HEAD_DIM:(hh + 1) * V_HEAD_DIM, :] = acc_ref[hh] / l_ref[hh]


def _out_kernel(yt_ref, gm_ref, yr_ref, x_ref, gmla_ref, wout_ref, gfin_ref, o_ref):
    ym = yt_ref[0].T
    ym = _rms(ym, gmla_ref[...]) * gm_ref[0].astype(f32)
    y = jnp.concatenate([ym.astype(bf16), yr_ref[0]], axis=-1)
    xo = x_ref[0] + jnp.dot(y, wout_ref[...], preferred_element_type=f32)
    o_ref[0] = _rms(xo, gfin_ref[...])


def _prepare_weights(w_in, w_q_b, w_kv_b, w_rg_a, w_rg_x):
    offs = [0, 256, 384, 416, 928, 1440, 1952]
    w_qlat, w_kvlat, w_krope, w_gm, w_xr, w_gr = (
        w_in[:, offs[i]:offs[i + 1]] for i in range(6))
    kpe_blk = jnp.zeros((D_MODEL, HEAD_PAD), w_in.dtype)
    kpe_blk = kpe_blk.at[:, QK_NOPE_DIM:QK_HEAD_DIM].set(w_krope)
    w_in_r = jnp.concatenate([w_qlat, w_kvlat, kpe_blk, w_gm, w_xr, w_gr], axis=1)

    wq = w_q_b.reshape(Q_LORA_RANK, MLA_HEADS, QK_HEAD_DIM)
    wq = jnp.pad(wq, ((0, 0), (0, 0), (0, HEAD_PAD - QK_HEAD_DIM)))
    wq = wq.reshape(Q_LORA_RANK, MLA_HEADS * HEAD_PAD)

    wkv = w_kv_b.reshape(KV_LORA_RANK, MLA_HEADS, QK_NOPE_DIM + V_HEAD_DIM)
    wk = jnp.pad(wkv[:, :, :QK_NOPE_DIM], ((0, 0), (0, 0), (0, HEAD_PAD - QK_NOPE_DIM)))
    wk = wk.reshape(KV_LORA_RANK, MLA_HEADS * HEAD_PAD)
    wv = wkv[:, :, QK_NOPE_DIM:].reshape(KV_LORA_RANK, MLA_WIDTH)
    wkv_r = jnp.concatenate([wk, wv], axis=1)

    eye = jnp.eye(RNN_BLOCKS, dtype=w_rg_a.dtype)

    def block_diag(w):
        return jnp.einsum('ncd,nm->ncmd', w, eye).reshape(RNN_WIDTH, RNN_WIDTH)

    wgate = jnp.concatenate([block_diag(w_rg_a), block_diag(w_rg_x)], axis=1)
    return (w_in_r.astype(bf16), wq.astype(bf16), wkv_r.astype(bf16), wgate.astype(bf16))


def _rope_tables(seq):
    pos = jnp.arange(seq, dtype=f32)
    inv_freq = ROPE_THETA ** (-jnp.arange(0, QK_ROPE_DIM, 2, dtype=f32) / QK_ROPE_DIM)
    ang = pos[:, None] * inv_freq[None, :]
    cos, sin = jnp.cos(ang), jnp.sin(ang)
    half = QK_ROPE_DIM // 2
    zeros = jnp.zeros((seq, half), f32)
    pad_l = jnp.zeros((seq, QK_NOPE_DIM), f32)
    pad_r = jnp.zeros((seq, HEAD_PAD - QK_HEAD_DIM), f32)
    c = jnp.concatenate([jnp.ones((seq, QK_NOPE_DIM), f32), cos, cos, pad_r], axis=1)
    s1 = jnp.concatenate([pad_l, -sin, zeros, pad_r], axis=1)
    s2 = jnp.concatenate([pad_l, zeros, sin, pad_r], axis=1)
    return c, s1, s2


def _const_spec(shape):
    return pl.BlockSpec(shape, lambda *_: (0,) * len(shape))


def kernel(x, norm_in_g, w_in, q_norm_g, w_q_b, kv_norm_g, w_kv_b, conv_w, conv_b,
           w_rg_a, b_rg_a, w_rg_x, b_rg_x, lru_param, out_norm_mla_g, out_norm_rnn_g,
           w_out, final_norm_g):
    b, s, _ = x.shape
    depth = w_in.shape[0]
    ts = SEQ_TILE
    tq = ATT_BLOCK
    assert s % ts == 0 and ts % tq == 0 and tq % CHUNK == 0
    rc, rs1, rs2 = _rope_tables(s)
    row = lambda v: v.reshape(1, -1).astype(f32)
    params = pltpu.CompilerParams(
        dimension_semantics=("arbitrary", "arbitrary"), vmem_limit_bytes=VMEM_LIMIT)

    assert depth == 1, "the final norm is fused into the single layer's output kernel"
    for l in range(depth):
        w_in_r, wq, wkv, wgate = _prepare_weights(
            w_in[l], w_q_b[l], w_kv_b[l], w_rg_a[l], w_rg_x[l])
        bgate = jnp.concatenate([b_rg_a[l], b_rg_x[l]]).reshape(1, -1)

        seq_spec = lambda w: pl.BlockSpec((1, ts, w), lambda bi, si: (bi, si, 0))
        tab_spec = pl.BlockSpec((ts, HEAD_PAD), lambda bi, si: (si, 0))
        head_spec = pl.BlockSpec((1, MLA_HEADS, ts, HEAD_PAD), lambda bi, si: (bi, 0, si, 0))
        qt, k, vt, gm, yr = pl.pallas_call(
            _proj_kernel,
            grid=(b, s // ts),
            in_specs=[
                seq_spec(D_MODEL), _const_spec((1, D_MODEL)), _const_spec((D_MODEL, Z_WIDTH)),
                _const_spec((1, Q_LORA_RANK)), _const_spec((Q_LORA_RANK, MLA_HEADS * HEAD_PAD)),
                _const_spec((1, KV_LORA_RANK)),
                _const_spec((KV_LORA_RANK, MLA_HEADS * HEAD_PAD + MLA_WIDTH)),
                tab_spec, tab_spec, tab_spec,
                _const_spec((CONV_WIDTH, RNN_WIDTH)), _const_spec((1, RNN_WIDTH)),
                _const_spec((RNN_WIDTH, 2 * RNN_WIDTH)), _const_spec((1, 2 * RNN_WIDTH)),
                _const_spec((1, RNN_WIDTH)), _const_spec((1, RNN_WIDTH)),
            ],
            out_specs=[
                pl.BlockSpec((1, ts // tq, MLA_HEADS, HEAD_PAD, tq),
                             lambda bi, si: (bi, si, 0, 0, 0)),
                head_spec,
                pl.BlockSpec((1, ts // tq, MLA_HEADS, V_HEAD_DIM, tq),
                             lambda bi, si: (bi, si, 0, 0, 0)),
                seq_spec(MLA_WIDTH), seq_spec(RNN_WIDTH),
            ],
            out_shape=[
                jax.ShapeDtypeStruct((b, s // tq, MLA_HEADS, HEAD_PAD, tq), bf16),
                jax.ShapeDtypeStruct((b, MLA_HEADS, s, HEAD_PAD), bf16),
                jax.ShapeDtypeStruct((b, s // tq, MLA_HEADS, V_HEAD_DIM, tq), bf16),
                jax.ShapeDtypeStruct((b, s, MLA_WIDTH), bf16),
                jax.ShapeDtypeStruct((b, s, RNN_WIDTH), bf16),
            ],
            scratch_shapes=[pltpu.VMEM((ts + 2 * SUBLANES, RNN_WIDTH), f32),
                            pltpu.VMEM((1, RNN_WIDTH), f32)],
            compiler_params=params,
            name="proj_rnn",
        )(x, row(norm_in_g[l]), w_in_r, row(q_norm_g[l]), wq, row(kv_norm_g[l]), wkv,
          rc, rs1, rs2, conv_w[l].astype(f32), row(conv_b[l]), wgate, bgate.astype(f32),
          row(lru_param[l]), row(out_norm_rnn_g[l]))

        yt = pl.pallas_call(
            _attn_kernel,
            grid=(b, s // tq),
            in_specs=[
                pl.BlockSpec((1, 1, MLA_HEADS, HEAD_PAD, tq), lambda bi, qi: (bi, qi, 0, 0, 0)),
                pl.BlockSpec((1, MLA_HEADS, s, HEAD_PAD), lambda bi, qi: (bi, 0, 0, 0)),
                pl.BlockSpec((1, s // tq, MLA_HEADS, V_HEAD_DIM, tq),
                             lambda bi, qi: (bi, 0, 0, 0, 0)),
            ],
            out_specs=pl.BlockSpec((1, MLA_WIDTH, tq), lambda bi, qi: (bi, 0, qi)),
            out_shape=jax.ShapeDtypeStruct((b, MLA_WIDTH, s), f32),
            scratch_shapes=[pltpu.VMEM((MLA_HEADS, tq, tq), f32),
                            pltpu.VMEM((MLA_HEADS, tq, tq), f32),
                            pltpu.VMEM((MLA_HEADS, 1, tq), f32),
                            pltpu.VMEM((MLA_HEADS, 1, tq), f32),
                            pltpu.VMEM((MLA_HEADS, V_HEAD_DIM, tq), f32)],
            compiler_params=params,
            name="mla_attention",
        )(qt, k, vt)

        x = pl.pallas_call(
            _out_kernel,
            grid=(b, s // ts),
            in_specs=[
                pl.BlockSpec((1, MLA_WIDTH, ts), lambda bi, si: (bi, 0, si)),
                seq_spec(MLA_WIDTH), seq_spec(RNN_WIDTH), seq_spec(D_MODEL),
                _const_spec((1, MLA_WIDTH)), _const_spec((D_MODEL, D_MODEL)),
                _const_spec((1, D_MODEL)),
            ],
            out_specs=seq_spec(D_MODEL),
            out_shape=jax.ShapeDtypeStruct((b, s, D_MODEL), f32),
            compiler_params=params,
            name="out_proj",
        )(yt, gm, yr, x, row(out_norm_mla_g[l]), w_out[l].astype(bf16),
          row(final_norm_g))
    return x
```

```python
import math

import jax
import jax.numpy as jnp
from jax import lax
from jax.experimental import pallas as pl
from jax.experimental.pallas import tpu as pltpu

D_MODEL = 1024
CHUNK = 64
MLA_HEADS = 8
QK_NOPE_DIM = 64
QK_ROPE_DIM = 32
QK_HEAD_DIM = QK_NOPE_DIM + QK_ROPE_DIM
V_HEAD_DIM = 64
MLA_WIDTH = MLA_HEADS * V_HEAD_DIM
Q_LORA_RANK = 256
KV_LORA_RANK = 128
ROPE_THETA = 10000.0
RNN_WIDTH = 512
RNN_BLOCKS = 8
RNN_BLOCK_DIM = RNN_WIDTH // RNN_BLOCKS
CONV_WIDTH = 4
LRU_C = 8.0
NORM_EPS = 1e-6

LANES = 128
SUBLANES = 8
MXU_DIM = 256
HEAD_PAD = LANES
Z_WIDTH = 2048
SEQ_TILE = 512
ATT_BLOCK = MXU_DIM
VMEM_LIMIT = 56 * 1024 * 1024

Z_Q, Z_KV, Z_KPE, Z_GM, Z_XR, Z_GR = 0, 256, 384, 512, 1024, 1536
Q_NOPE, Q_PE, Q_ROT = 0, MLA_HEADS * QK_NOPE_DIM, MLA_HEADS * (QK_NOPE_DIM + QK_ROPE_DIM)
Q_PE_WIDTH = MLA_HEADS * QK_ROPE_DIM
GATE_GROUP = MXU_DIM

MASK_VALUE = -0.5 * float(jnp.finfo(jnp.float32).max)

f32 = jnp.float32
bf16 = jnp.bfloat16


def _rms(x, g):
    ms = jnp.mean(x * x, axis=-1, keepdims=True)
    return x * lax.rsqrt(ms + NORM_EPS) * g


def _sigmoid(x):
    return 0.5 * jnp.tanh(0.5 * x) + 0.5


def _silu(x):
    hx = 0.5 * x
    return hx * jnp.tanh(hx) + hx


def _shift_rows(x, tail, j):
    sh = pltpu.roll(x, j, 0)
    th = pltpu.roll(tail, j, 0)
    sub = lax.broadcasted_iota(jnp.int32, tail.shape, 0)
    head = jnp.where(sub < j, th, sh[:SUBLANES])
    return jnp.concatenate([head, sh[SUBLANES:]], axis=0)


def _proj_kernel(x_ref, gin_ref, win_ref, qg_ref, wq_ref, kvg_ref, wkv_ref,
                 cq_ref, sq_ref, ck_ref, sk_ref, convw_ref, convb_ref, wgate_ref,
                 bgate_ref, lru_ref, grnn_ref,
                 qt_out, k_out, vt_out, gm_out, yr_out,
                 xr_tail, h_carry):
    s = pl.program_id(1)
    ts = x_ref.shape[1]
    x = x_ref[0]
    h = _rms(x, gin_ref[...]).astype(bf16)
    z = jnp.dot(h, win_ref[...], preferred_element_type=f32)

    scale = math.log2(math.e) / math.sqrt(QK_HEAD_DIM)
    qn = _rms(z[:, Z_Q:Z_Q + Q_LORA_RANK], qg_ref[...]).astype(bf16)
    q = jnp.dot(qn, wq_ref[...], preferred_element_type=f32)
    q_nope_t = (q[:, Q_NOPE:Q_PE] * scale).T
    q_pe = q[:, Q_PE:Q_ROT] * cq_ref[...] + q[:, Q_ROT:] * sq_ref[...]
    q_pe_t = q_pe.T
    pad = jnp.zeros((HEAD_PAD - QK_HEAD_DIM, ATT_BLOCK), bf16)
    for j in range(ts // ATT_BLOCK):
        cols = slice(j * ATT_BLOCK, (j + 1) * ATT_BLOCK)
        for hh in range(MLA_HEADS):
            qt_out[0, j, hh, :QK_NOPE_DIM, :] = (
                q_nope_t[hh * QK_NOPE_DIM:(hh + 1) * QK_NOPE_DIM, cols].astype(bf16))
            qt_out[0, j, hh, QK_NOPE_DIM:QK_HEAD_DIM, :] = (
                q_pe_t[hh * QK_ROPE_DIM:(hh + 1) * QK_ROPE_DIM, cols].astype(bf16))
            qt_out[0, j, hh, QK_HEAD_DIM:, :] = pad

    kvn = _rms(z[:, Z_KV:Z_KV + KV_LORA_RANK], kvg_ref[...]).astype(bf16)
    kv = jnp.dot(kvn, wkv_ref[...], preferred_element_type=f32)
    kblk = z[:, Z_KPE:Z_KPE + HEAD_PAD]
    kpe = kblk * ck_ref[...] + pltpu.roll(kblk, LANES - QK_ROPE_DIM, 1) * sk_ref[...]
    for hh in range(MLA_HEADS):
        k_out[0, hh] = (kv[:, hh * HEAD_PAD:(hh + 1) * HEAD_PAD] + kpe).astype(bf16)
    vt = kv[:, MLA_HEADS * HEAD_PAD:].T
    for j in range(ts // ATT_BLOCK):
        blk = vt[:, j * ATT_BLOCK:(j + 1) * ATT_BLOCK]
        vt_out[0, j] = blk.reshape(MLA_HEADS, V_HEAD_DIM, ATT_BLOCK).astype(bf16)

    gm_out[0] = _silu(z[:, Z_GM:Z_GM + MLA_WIDTH]).astype(bf16)

    @pl.when(s == 0)
    def _():
        xr_tail[...] = jnp.zeros(xr_tail.shape, f32)
        h_carry[...] = jnp.zeros(h_carry.shape, f32)

    xin = z[:, Z_XR:Z_XR + RNN_WIDTH]
    tail = xr_tail[...]
    xr = convb_ref[...] + convw_ref[CONV_WIDTH - 1:CONV_WIDTH, :] * xin
    for j in range(1, CONV_WIDTH):
        k = CONV_WIDTH - 1 - j
        xr = xr + convw_ref[k:k + 1, :] * _shift_rows(xin, tail, j)
    xr_tail[...] = xin[ts - SUBLANES:, :]

    xb = xr.astype(bf16)
    ga, gx = [], []
    for g in range(RNN_WIDTH // GATE_GROUP):
        gg = jnp.dot(xb[:, g * GATE_GROUP:(g + 1) * GATE_GROUP], wgate_ref[g],
                     preferred_element_type=f32)
        ga.append(gg[:, :GATE_GROUP])
        gx.append(gg[:, GATE_GROUP:])
    r = _sigmoid(jnp.concatenate(ga, axis=1) + bgate_ref[0:1, :])
    i = _sigmoid(jnp.concatenate(gx, axis=1) + bgate_ref[1:2, :])
    nl = -lru_ref[...]
    softplus = jnp.maximum(nl, 0.0) + jnp.log1p(jnp.exp(-jnp.abs(nl)))
    a = jnp.exp2(r * ((-LRU_C * math.log2(math.e)) * softplus))
    m2 = jnp.maximum(1.0 - a * a, 1e-12)
    mult = m2 * lax.rsqrt(m2)
    row = lax.broadcasted_iota(jnp.int32, (ts, RNN_WIDTH), 0)
    mult = jnp.where((row == 0) & (s == 0), 1.0, mult)
    u = mult * (i * xr)

    groups = ts // SUBLANES
    a3 = a.reshape(groups, SUBLANES, RNN_WIDTH)
    u3 = u.reshape(groups, SUBLANES, RNN_WIDTH)
    sub = lax.broadcasted_iota(jnp.int32, (1, SUBLANES, RNN_WIDTH), 1)
    for d in (1, 2, 4):
        keep = sub >= d
        u3 = jnp.where(keep, a3 * pltpu.roll(u3, d, 1) + u3, u3)
        a3 = jnp.where(keep, a3 * pltpu.roll(a3, d, 1), a3)
    carry = h_carry[...]
    outs = []
    for g in range(groups):
        hg = u3[g] + a3[g] * carry
        outs.append(hg)
        carry = hg[SUBLANES - 1:SUBLANES, :]
    h_carry[...] = carry
    y = jnp.concatenate(outs, axis=0)

    yr_out[0] = (_rms(y, grnn_ref[...]) * _silu(z[:, Z_GR:Z_GR + RNN_WIDTH])).astype(bf16)


def _attn_kernel(qt_ref, k_ref, vt_ref, o_ref, s_even, s_odd, m_ref, l_ref, acc_ref):
    qi = pl.program_id(1)
    tq = ATT_BLOCK

    m_ref[...] = jnp.full(m_ref.shape, MASK_VALUE, f32)
    l_ref[...] = jnp.zeros(l_ref.shape, f32)
    acc_ref[...] = jnp.zeros(acc_ref.shape, f32)

    def scores(j, s_dst):
        for hh in range(MLA_HEADS):
            kb = k_ref[0, hh, pl.ds(pl.multiple_of(j * tq, tq), tq), :]
            s_dst[hh] = jnp.dot(kb, qt_ref[0, 0, hh], preferred_element_type=f32)

    def update(j, s_src, masked):
        for hh in range(MLA_HEADS):
            st = s_src[hh]
            if masked:
                kc = lax.broadcasted_iota(jnp.int32, st.shape, 0) // CHUNK
                qc = lax.broadcasted_iota(jnp.int32, st.shape, 1) // CHUNK
                st = jnp.where(kc <= qc, st, MASK_VALUE)
            m_prev = m_ref[hh]
            m_new = jnp.maximum(m_prev, jnp.max(st, axis=0, keepdims=True))
            alpha = jnp.exp2(m_prev - m_new)
            p = jnp.exp2(st - m_new)
            l_ref[hh] = alpha * l_ref[hh] + jnp.sum(p, axis=0, keepdims=True)
            pv = jnp.dot(vt_ref[0, j, hh], p.astype(bf16), preferred_element_type=f32)
            acc_ref[hh] = alpha * acc_ref[hh] + pv
            m_ref[hh] = m_new

    scores(0, s_even)

    def body(j, carry):
        @pl.when(lax.rem(j, 2) == 0)
        def _():
            scores(j + 1, s_odd)
            update(j, s_even, False)

        @pl.when(lax.rem(j, 2) == 1)
        def _():
            scores(j + 1, s_even)
            update(j, s_odd, False)
        return carry

    lax.fori_loop(0, qi, body, 0)

    @pl.when(lax.rem(qi, 2) == 0)
    def _():
        update(qi, s_even, True)

    @pl.when(lax.rem(qi, 2) == 1)
    def _():
        update(qi, s_odd, True)

    for hh in range(MLA_HEADS):
        o_ref[0, hh * V_HEAD_DIM:(hh + 1) * V_HEAD_DIM, :] = acc_ref[hh] / l_ref[hh]


def _out_kernel(yt_ref, gm_ref, yr_ref, x_ref, gmla_ref, wout_ref, gfin_ref, o_ref):
    ym = yt_ref[0].T
    ym = _rms(ym, gmla_ref[...]) * gm_ref[0].astype(f32)
    y = jnp.concatenate([ym.astype(bf16), yr_ref[0]], axis=-1)
    xo = x_ref[0] + jnp.dot(y, wout_ref[...], preferred_element_type=f32)
    o_ref[0] = _rms(xo, gfin_ref[...])


def _rot_half(w):
    half = QK_ROPE_DIM // 2
    return jnp.concatenate([-w[..., half:], w[..., :half]], axis=-1)


def _prepare_weights(w_in, w_q_b, w_kv_b, w_rg_a, w_rg_x):
    offs = [0, 256, 384, 416, 928, 1440, 1952]
    w_qlat, w_kvlat, w_krope, w_gm, w_xr, w_gr = (
        w_in[:, offs[i]:offs[i + 1]] for i in range(6))
    kpe_blk = jnp.concatenate(
        [jnp.zeros((D_MODEL, QK_NOPE_DIM), w_in.dtype), w_krope, _rot_half(w_krope)], axis=1)
    w_in_r = jnp.concatenate([w_qlat, w_kvlat, kpe_blk, w_gm, w_xr, w_gr], axis=1)

    wq = w_q_b.reshape(Q_LORA_RANK, MLA_HEADS, QK_HEAD_DIM)
    wq_nope = wq[:, :, :QK_NOPE_DIM].reshape(Q_LORA_RANK, -1)
    wq_pe = wq[:, :, QK_NOPE_DIM:]
    wq_r = jnp.concatenate([wq_nope, wq_pe.reshape(Q_LORA_RANK, -1),
                            _rot_half(wq_pe).reshape(Q_LORA_RANK, -1)], axis=1)

    wkv = w_kv_b.reshape(KV_LORA_RANK, MLA_HEADS, QK_NOPE_DIM + V_HEAD_DIM)
    wk = jnp.pad(wkv[:, :, :QK_NOPE_DIM], ((0, 0), (0, 0), (0, HEAD_PAD - QK_NOPE_DIM)))
    wk = wk.reshape(KV_LORA_RANK, MLA_HEADS * HEAD_PAD)
    wv = wkv[:, :, QK_NOPE_DIM:].reshape(KV_LORA_RANK, MLA_WIDTH)
    wkv_r = jnp.concatenate([wk, wv], axis=1)

    per = GATE_GROUP // RNN_BLOCK_DIM
    eye = jnp.eye(per, dtype=w_rg_a.dtype)

    def block_diag(w):
        w = w.reshape(RNN_BLOCKS // per, per, RNN_BLOCK_DIM, RNN_BLOCK_DIM)
        return jnp.einsum('gncd,nm->gncmd', w, eye).reshape(-1, GATE_GROUP, GATE_GROUP)

    wgate = jnp.concatenate([block_diag(w_rg_a), block_diag(w_rg_x)], axis=2)
    return (w_in_r.astype(bf16), wq_r.astype(bf16), wkv_r.astype(bf16), wgate.astype(bf16))


def _rope_tables(seq):
    pos = jnp.arange(seq, dtype=f32)
    inv_freq = ROPE_THETA ** (-jnp.arange(0, QK_ROPE_DIM, 2, dtype=f32) / QK_ROPE_DIM)
    ang = pos[:, None] * inv_freq[None, :]
    cos = jnp.tile(jnp.cos(ang), (1, 2))
    sin = jnp.tile(jnp.sin(ang), (1, 2))
    scale = math.log2(math.e) / math.sqrt(QK_HEAD_DIM)
    cq = jnp.tile(cos, (1, MLA_HEADS)) * scale
    sq = jnp.tile(sin, (1, MLA_HEADS)) * scale
    pads = ((0, 0), (QK_NOPE_DIM, HEAD_PAD - QK_HEAD_DIM))
    return cq, sq, jnp.pad(cos, pads), jnp.pad(sin, pads)


def _const_spec(shape):
    return pl.BlockSpec(shape, lambda *_: (0,) * len(shape))


def kernel(x, norm_in_g, w_in, q_norm_g, w_q_b, kv_norm_g, w_kv_b, conv_w, conv_b,
           w_rg_a, b_rg_a, w_rg_x, b_rg_x, lru_param, out_norm_mla_g, out_norm_rnn_g,
           w_out, final_norm_g):
    b, s, _ = x.shape
    ts = SEQ_TILE
    tq = ATT_BLOCK
    assert s % ts == 0 and ts % tq == 0 and tq % CHUNK == 0
    assert w_in.shape[0] == 1, "the final norm is fused into the single layer's output kernel"
    cq, sq, ck, sk = _rope_tables(s)
    row = lambda v: v.reshape(1, -1).astype(f32)
    params = pltpu.CompilerParams(
        dimension_semantics=("arbitrary", "arbitrary"), vmem_limit_bytes=VMEM_LIMIT)

    w_in_r, wq, wkv, wgate = _prepare_weights(w_in[0], w_q_b[0], w_kv_b[0], w_rg_a[0], w_rg_x[0])
    bgate = jnp.stack([b_rg_a[0], b_rg_x[0]]).astype(f32)

    seq_spec = lambda w: pl.BlockSpec((1, ts, w), lambda bi, si: (bi, si, 0))
    tab_spec = lambda w: pl.BlockSpec((ts, w), lambda bi, si: (si, 0))
    qt, k, vt, gm, yr = pl.pallas_call(
        _proj_kernel,
        grid=(b, s // ts),
        in_specs=[
            seq_spec(D_MODEL), _const_spec((1, D_MODEL)), _const_spec((D_MODEL, Z_WIDTH)),
            _const_spec((1, Q_LORA_RANK)), _const_spec(wq.shape),
            _const_spec((1, KV_LORA_RANK)), _const_spec(wkv.shape),
            tab_spec(Q_PE_WIDTH), tab_spec(Q_PE_WIDTH), tab_spec(HEAD_PAD), tab_spec(HEAD_PAD),
            _const_spec((CONV_WIDTH, RNN_WIDTH)), _const_spec((1, RNN_WIDTH)),
            _const_spec(wgate.shape), _const_spec((2, RNN_WIDTH)),
            _const_spec((1, RNN_WIDTH)), _const_spec((1, RNN_WIDTH)),
        ],
        out_specs=[
            pl.BlockSpec((1, ts // tq, MLA_HEADS, HEAD_PAD, tq), lambda bi, si: (bi, si, 0, 0, 0)),
            pl.BlockSpec((1, MLA_HEADS, ts, HEAD_PAD), lambda bi, si: (bi, 0, si, 0)),
            pl.BlockSpec((1, ts // tq, MLA_HEADS, V_HEAD_DIM, tq),
                         lambda bi, si: (bi, si, 0, 0, 0)),
            seq_spec(MLA_WIDTH), seq_spec(RNN_WIDTH),
        ],
        out_shape=[
            jax.ShapeDtypeStruct((b, s // tq, MLA_HEADS, HEAD_PAD, tq), bf16),
            jax.ShapeDtypeStruct((b, MLA_HEADS, s, HEAD_PAD), bf16),
            jax.ShapeDtypeStruct((b, s // tq, MLA_HEADS, V_HEAD_DIM, tq), bf16),
            jax.ShapeDtypeStruct((b, s, MLA_WIDTH), bf16),
            jax.ShapeDtypeStruct((b, s, RNN_WIDTH), bf16),
        ],
        scratch_shapes=[pltpu.VMEM((SUBLANES, RNN_WIDTH), f32),
                        pltpu.VMEM((1, RNN_WIDTH), f32)],
        compiler_params=params,
        name="proj_rnn",
    )(x, row(norm_in_g[0]), w_in_r, row(q_norm_g[0]), wq, row(kv_norm_g[0]), wkv,
      cq, sq, ck, sk, conv_w[0].astype(f32), row(conv_b[0]), wgate, bgate,
      row(lru_param[0]), row(out_norm_rnn_g[0]))

    yt = pl.pallas_call(
        _attn_kernel,
        grid=(b, s // tq),
        in_specs=[
            pl.BlockSpec((1, 1, MLA_HEADS, HEAD_PAD, tq), lambda bi, qi: (bi, qi, 0, 0, 0)),
            pl.BlockSpec((1, MLA_HEADS, s, HEAD_PAD), lambda bi, qi: (bi, 0, 0, 0)),
            pl.BlockSpec((1, s // tq, MLA_HEADS, V_HEAD_DIM, tq),
                         lambda bi, qi: (bi, 0, 0, 0, 0)),
        ],
        out_specs=pl.BlockSpec((1, MLA_WIDTH, tq), lambda bi, qi: (bi, 0, qi)),
        out_shape=jax.ShapeDtypeStruct((b, MLA_WIDTH, s), f32),
        scratch_shapes=[pltpu.VMEM((MLA_HEADS, tq, tq), f32),
                        pltpu.VMEM((MLA_HEADS, tq, tq), f32),
                        pltpu.VMEM((MLA_HEADS, 1, tq), f32),
                        pltpu.VMEM((MLA_HEADS, 1, tq), f32),
                        pltpu.VMEM((MLA_HEADS, V_HEAD_DIM, tq), f32)],
        compiler_params=params,
        name="mla_attention",
    )(qt, k, vt)

    return pl.pallas_call(
        _out_kernel,
        grid=(b, s // ts),
        in_specs=[
            pl.BlockSpec((1, MLA_WIDTH, ts), lambda bi, si: (bi, 0, si)),
            seq_spec(MLA_WIDTH), seq_spec(RNN_WIDTH), seq_spec(D_MODEL),
            _const_spec((1, MLA_WIDTH)), _const_spec((D_MODEL, D_MODEL)),
            _const_spec((1, D_MODEL)),
        ],
        out_specs=seq_spec(D_MODEL),
        out_shape=jax.ShapeDtypeStruct((b, s, D_MODEL), f32),
        compiler_params=params,
        name="out_proj",
    )(yt, gm, yr, x, row(out_norm_mla_g[0]), w_out[0].astype(bf16), row(final_norm_g))
```

```python
import math

import jax
import jax.numpy as jnp
from jax import lax
from jax.experimental import pallas as pl
from jax.experimental.pallas import tpu as pltpu

D_MODEL = 1024
CHUNK = 64
MLA_HEADS = 8
QK_NOPE_DIM = 64
QK_ROPE_DIM = 32
QK_HEAD_DIM = QK_NOPE_DIM + QK_ROPE_DIM
V_HEAD_DIM = 64
MLA_WIDTH = MLA_HEADS * V_HEAD_DIM
Q_LORA_RANK = 256
KV_LORA_RANK = 128
ROPE_THETA = 10000.0
RNN_WIDTH = 512
RNN_BLOCKS = 8
RNN_BLOCK_DIM = RNN_WIDTH // RNN_BLOCKS
CONV_WIDTH = 4
LRU_C = 8.0
NORM_EPS = 1e-6

LANES = 128
SUBLANES = 8
MXU_DIM = 256
HEAD_PAD = LANES
V_EXT = V_HEAD_DIM + 16
Z_WIDTH = 2048
SEQ_TILE = 512
ATT_BLOCK = MXU_DIM
VMEM_LIMIT = 56 * 1024 * 1024

Z_Q, Z_KV, Z_KPE, Z_GM, Z_XR, Z_GR = 0, 256, 384, 512, 1024, 1536
Q_NOPE, Q_PE, Q_ROT = 0, MLA_HEADS * QK_NOPE_DIM, MLA_HEADS * (QK_NOPE_DIM + QK_ROPE_DIM)
Q_PE_WIDTH = MLA_HEADS * QK_ROPE_DIM
GATE_GROUP = MXU_DIM

MASK_VALUE = -0.5 * float(jnp.finfo(jnp.float32).max)

f32 = jnp.float32
bf16 = jnp.bfloat16


def _rms(x, g):
    ms = jnp.mean(x * x, axis=-1, keepdims=True)
    return x * lax.rsqrt(ms + NORM_EPS) * g


def _sigmoid(x):
    return 0.5 * jnp.tanh(0.5 * x) + 0.5


def _silu(x):
    hx = 0.5 * x
    return hx * jnp.tanh(hx) + hx


def _shift_rows(x, tail, j):
    sh = pltpu.roll(x, j, 0)
    th = pltpu.roll(tail, j, 0)
    sub = lax.broadcasted_iota(jnp.int32, tail.shape, 0)
    head = jnp.where(sub < j, th, sh[:SUBLANES])
    return jnp.concatenate([head, sh[SUBLANES:]], axis=0)


def _proj_kernel(x_ref, gin_ref, win_ref, qg_ref, wq_ref, kvg_ref, wkv_ref,
                 cq_ref, sq_ref, ck_ref, sk_ref, convw_ref, convb_ref, wgate_ref,
                 bgate_ref, lru_ref, grnn_ref,
                 qt_out, k_out, vt_out, gm_out, yr_out,
                 xr_tail, h_carry):
    s = pl.program_id(1)
    ts = x_ref.shape[1]
    x = x_ref[0]
    h = _rms(x, gin_ref[...]).astype(bf16)
    z = jnp.dot(h, win_ref[...], preferred_element_type=f32)

    scale = math.log2(math.e) / math.sqrt(QK_HEAD_DIM)
    qn = _rms(z[:, Z_Q:Z_Q + Q_LORA_RANK], qg_ref[...]).astype(bf16)
    q = jnp.dot(qn, wq_ref[...], preferred_element_type=f32)
    q_nope_t = (q[:, Q_NOPE:Q_PE] * scale).T
    q_pe = q[:, Q_PE:Q_ROT] * cq_ref[...] + q[:, Q_ROT:] * sq_ref[...]
    q_pe_t = q_pe.T
    pad = jnp.zeros((HEAD_PAD - QK_HEAD_DIM, ATT_BLOCK), bf16)
    for j in range(ts // ATT_BLOCK):
        cols = slice(j * ATT_BLOCK, (j + 1) * ATT_BLOCK)
        for hh in range(MLA_HEADS):
            qt_out[0, j, hh, :QK_NOPE_DIM, :] = (
                q_nope_t[hh * QK_NOPE_DIM:(hh + 1) * QK_NOPE_DIM, cols].astype(bf16))
            qt_out[0, j, hh, QK_NOPE_DIM:QK_HEAD_DIM, :] = (
                q_pe_t[hh * QK_ROPE_DIM:(hh + 1) * QK_ROPE_DIM, cols].astype(bf16))
            qt_out[0, j, hh, QK_HEAD_DIM:, :] = pad

    kvn = _rms(z[:, Z_KV:Z_KV + KV_LORA_RANK], kvg_ref[...]).astype(bf16)
    kv = jnp.dot(kvn, wkv_ref[...], preferred_element_type=f32)
    kblk = z[:, Z_KPE:Z_KPE + HEAD_PAD]
    kpe = kblk * ck_ref[...] + pltpu.roll(kblk, LANES - QK_ROPE_DIM, 1) * sk_ref[...]
    for hh in range(MLA_HEADS):
        k_out[0, hh] = (kv[:, hh * HEAD_PAD:(hh + 1) * HEAD_PAD] + kpe).astype(bf16)
    vt = kv[:, MLA_HEADS * HEAD_PAD:].T
    ones_row = lax.broadcasted_iota(jnp.int32, (MLA_HEADS, V_EXT - V_HEAD_DIM, ATT_BLOCK), 1) == 0
    for j in range(ts // ATT_BLOCK):
        blk = vt[:, j * ATT_BLOCK:(j + 1) * ATT_BLOCK]
        vt_out[0, j, :, :V_HEAD_DIM, :] = (
            blk.reshape(MLA_HEADS, V_HEAD_DIM, ATT_BLOCK).astype(bf16))
        vt_out[0, j, :, V_HEAD_DIM:, :] = jnp.where(ones_row, 1.0, 0.0).astype(bf16)

    gm_out[0] = _silu(z[:, Z_GM:Z_GM + MLA_WIDTH]).astype(bf16)

    @pl.when(s == 0)
    def _():
        xr_tail[...] = jnp.zeros(xr_tail.shape, f32)
        h_carry[...] = jnp.zeros(h_carry.shape, f32)

    xin = z[:, Z_XR:Z_XR + RNN_WIDTH]
    tail = xr_tail[...]
    xr = convb_ref[...] + convw_ref[CONV_WIDTH - 1:CONV_WIDTH, :] * xin
    for j in range(1, CONV_WIDTH):
        k = CONV_WIDTH - 1 - j
        xr = xr + convw_ref[k:k + 1, :] * _shift_rows(xin, tail, j)
    xr_tail[...] = xin[ts - SUBLANES:, :]

    xb = xr.astype(bf16)
    ga, gx = [], []
    for g in range(RNN_WIDTH // GATE_GROUP):
        gg = jnp.dot(xb[:, g * GATE_GROUP:(g + 1) * GATE_GROUP], wgate_ref[g],
                     preferred_element_type=f32)
        ga.append(gg[:, :GATE_GROUP])
        gx.append(gg[:, GATE_GROUP:])
    r = _sigmoid(jnp.concatenate(ga, axis=1) + bgate_ref[0:1, :])
    i = _sigmoid(jnp.concatenate(gx, axis=1) + bgate_ref[1:2, :])
    nl = -lru_ref[...]
    softplus = jnp.maximum(nl, 0.0) + jnp.log1p(jnp.exp(-jnp.abs(nl)))
    a = jnp.exp2(r * ((-LRU_C * math.log2(math.e)) * softplus))
    m2 = jnp.maximum(1.0 - a * a, 1e-12)
    mult = m2 * lax.rsqrt(m2)
    row = lax.broadcasted_iota(jnp.int32, (ts, RNN_WIDTH), 0)
    mult = jnp.where((row == 0) & (s == 0), 1.0, mult)
    u = mult * (i * xr)

    groups = ts // SUBLANES
    a3 = a.reshape(groups, SUBLANES, RNN_WIDTH)
    u3 = u.reshape(groups, SUBLANES, RNN_WIDTH)
    sub = lax.broadcasted_iota(jnp.int32, (1, SUBLANES, RNN_WIDTH), 1)
    for d in (1, 2, 4):
        keep = sub >= d
        u3 = jnp.where(keep, a3 * pltpu.roll(u3, d, 1) + u3, u3)
        a3 = jnp.where(keep, a3 * pltpu.roll(a3, d, 1), a3)
    carry = h_carry[...]
    outs = []
    for g in range(groups):
        hg = u3[g] + a3[g] * carry
        outs.append(hg)
        carry = hg[SUBLANES - 1:SUBLANES, :]
    h_carry[...] = carry
    y = jnp.concatenate(outs, axis=0)

    yr_out[0] = (_rms(y, grnn_ref[...]) * _silu(z[:, Z_GR:Z_GR + RNN_WIDTH])).astype(bf16)


def _attn_kernel(qt_ref, k_ref, vt_ref, o_ref,
                 s_even, s_odd, p_even, p_odd, a_even, a_odd, m_ref, acc_ref):
    qi = pl.program_id(1)
    tq = ATT_BLOCK

    m_ref[...] = jnp.full(m_ref.shape, MASK_VALUE, f32)
    acc_ref[...] = jnp.zeros(acc_ref.shape, f32)
    p_odd[...] = jnp.zeros(p_odd.shape, bf16)
    a_odd[...] = jnp.ones(a_odd.shape, f32)

    def qk(hh, j, s_dst):
        kb = k_ref[0, hh, pl.ds(pl.multiple_of(j * tq, tq), tq), :]
        s_dst[hh] = jnp.dot(kb, qt_ref[0, 0, hh], preferred_element_type=f32)

    def pv(hh, j, p_src, a_src):
        upd = jnp.dot(vt_ref[0, j, hh], p_src[hh], preferred_element_type=f32)
        acc_ref[hh] = a_src[hh] * acc_ref[hh] + upd

    def softmax(hh, s_src, p_dst, a_dst, masked):
        st = s_src[hh]
        if masked:
            kc = lax.broadcasted_iota(jnp.int32, st.shape, 0) // CHUNK
            qc = lax.broadcasted_iota(jnp.int32, st.shape, 1) // CHUNK
            st = jnp.where(kc <= qc, st, MASK_VALUE)
        m_prev = m_ref[hh]
        m_new = jnp.maximum(m_prev, jnp.max(st, axis=0, keepdims=True))
        a_dst[hh] = jnp.exp2(m_prev - m_new)
        p_dst[hh] = jnp.exp2(st - m_new).astype(bf16)
        m_ref[hh] = m_new

    def step(j, s_cur, s_nxt, p_cur, p_prv, a_cur, a_prv):
        for hh in range(MLA_HEADS):
            qk(hh, j + 1, s_nxt)
            pv(hh, jnp.maximum(j - 1, 0), p_prv, a_prv)
            softmax(hh, s_cur, p_cur, a_cur, False)

    def last(s_cur, p_cur, p_prv, a_cur, a_prv):
        for hh in range(MLA_HEADS):
            pv(hh, jnp.maximum(qi - 1, 0), p_prv, a_prv)
            softmax(hh, s_cur, p_cur, a_cur, True)
            pv(hh, qi, p_cur, a_cur)
            acc = acc_ref[hh]
            o_ref[0, hh * V_HEAD_DIM:(hh + 1) * V_HEAD_DIM, :] = (
                acc[:V_HEAD_DIM] / acc[V_HEAD_DIM:V_HEAD_DIM + 1])

    for hh in range(MLA_HEADS):
        qk(hh, 0, s_even)

    def body(j, carry):
        @pl.when(lax.rem(j, 2) == 0)
        def _():
            step(j, s_even, s_odd, p_even, p_odd, a_even, a_odd)

        @pl.when(lax.rem(j, 2) == 1)
        def _():
            step(j, s_odd, s_even, p_odd, p_even, a_odd, a_even)
        return carry

    lax.fori_loop(0, qi, body, 0)

    @pl.when(lax.rem(qi, 2) == 0)
    def _():
        last(s_even, p_even, p_odd, a_even, a_odd)

    @pl.when(lax.rem(qi, 2) == 1)
    def _():
        last(s_odd, p_odd, p_even, a_odd, a_even)


def _out_kernel(yt_ref, gm_ref, yr_ref, x_ref, gmla_ref, wout_ref, gfin_ref, o_ref):
    ym = yt_ref[0].T
    ym = _rms(ym, gmla_ref[...]) * gm_ref[0].astype(f32)
    y = jnp.concatenate([ym.astype(bf16), yr_ref[0]], axis=-1)
    xo = x_ref[0] + jnp.dot(y, wout_ref[...], preferred_element_type=f32)
    o_ref[0] = _rms(xo, gfin_ref[...])


def _rot_half(w):
    half = QK_ROPE_DIM // 2
    return jnp.concatenate([-w[..., half:], w[..., :half]], axis=-1)


def _prepare_weights(w_in, w_q_b, w_kv_b, w_rg_a, w_rg_x):
    offs = [0, 256, 384, 416, 928, 1440, 1952]
    w_qlat, w_kvlat, w_krope, w_gm, w_xr, w_gr = (
        w_in[:, offs[i]:offs[i + 1]] for i in range(6))
    kpe_blk = jnp.concatenate(
        [jnp.zeros((D_MODEL, QK_NOPE_DIM), w_in.dtype), w_krope, _rot_half(w_krope)], axis=1)
    w_in_r = jnp.concatenate([w_qlat, w_kvlat, kpe_blk, w_gm, w_xr, w_gr], axis=1)

    wq = w_q_b.reshape(Q_LORA_RANK, MLA_HEADS, QK_HEAD_DIM)
    wq_nope = wq[:, :, :QK_NOPE_DIM].reshape(Q_LORA_RANK, -1)
    wq_pe = wq[:, :, QK_NOPE_DIM:]
    wq_r = jnp.concatenate([wq_nope, wq_pe.reshape(Q_LORA_RANK, -1),
                            _rot_half(wq_pe).reshape(Q_LORA_RANK, -1)], axis=1)

    wkv = w_kv_b.reshape(KV_LORA_RANK, MLA_HEADS, QK_NOPE_DIM + V_HEAD_DIM)
    wk = jnp.pad(wkv[:, :, :QK_NOPE_DIM], ((0, 0), (0, 0), (0, HEAD_PAD - QK_NOPE_DIM)))
    wk = wk.reshape(KV_LORA_RANK, MLA_HEADS * HEAD_PAD)
    wv = wkv[:, :, QK_NOPE_DIM:].reshape(KV_LORA_RANK, MLA_WIDTH)
    wkv_r = jnp.concatenate([wk, wv], axis=1)

    per = GATE_GROUP // RNN_BLOCK_DIM
    eye = jnp.eye(per, dtype=w_rg_a.dtype)

    def block_diag(w):
        w = w.reshape(RNN_BLOCKS // per, per, RNN_BLOCK_DIM, RNN_BLOCK_DIM)
        return jnp.einsum('gncd,nm->gncmd', w, eye).reshape(-1, GATE_GROUP, GATE_GROUP)

    wgate = jnp.concatenate([block_diag(w_rg_a), block_diag(w_rg_x)], axis=2)
    return (w_in_r.astype(bf16), wq_r.astype(bf16), wkv_r.astype(bf16), wgate.astype(bf16))


def _rope_tables(seq):
    pos = jnp.arange(seq, dtype=f32)
    inv_freq = ROPE_THETA ** (-jnp.arange(0, QK_ROPE_DIM, 2, dtype=f32) / QK_ROPE_DIM)
    ang = pos[:, None] * inv_freq[None, :]
    cos = jnp.tile(jnp.cos(ang), (1, 2))
    sin = jnp.tile(jnp.sin(ang), (1, 2))
    scale = math.log2(math.e) / math.sqrt(QK_HEAD_DIM)
    cq = jnp.tile(cos, (1, MLA_HEADS)) * scale
    sq = jnp.tile(sin, (1, MLA_HEADS)) * scale
    pads = ((0, 0), (QK_NOPE_DIM, HEAD_PAD - QK_HEAD_DIM))
    return cq, sq, jnp.pad(cos, pads), jnp.pad(sin, pads)


def _const_spec(shape):
    return pl.BlockSpec(shape, lambda *_: (0,) * len(shape))


def kernel(x, norm_in_g, w_in, q_norm_g, w_q_b, kv_norm_g, w_kv_b, conv_w, conv_b,
           w_rg_a, b_rg_a, w_rg_x, b_rg_x, lru_param, out_norm_mla_g, out_norm_rnn_g,
           w_out, final_norm_g):
    b, s, _ = x.shape
    ts = SEQ_TILE
    tq = ATT_BLOCK
    assert s % ts == 0 and ts % tq == 0 and tq % CHUNK == 0
    assert w_in.shape[0] == 1, "the final norm is fused into the single layer's output kernel"
    cq, sq, ck, sk = _rope_tables(s)
    row = lambda v: v.reshape(1, -1).astype(f32)
    params = pltpu.CompilerParams(
        dimension_semantics=("arbitrary", "arbitrary"), vmem_limit_bytes=VMEM_LIMIT)

    w_in_r, wq, wkv, wgate = _prepare_weights(w_in[0], w_q_b[0], w_kv_b[0], w_rg_a[0], w_rg_x[0])
    bgate = jnp.stack([b_rg_a[0], b_rg_x[0]]).astype(f32)

    seq_spec = lambda w: pl.BlockSpec((1, ts, w), lambda bi, si: (bi, si, 0))
    tab_spec = lambda w: pl.BlockSpec((ts, w), lambda bi, si: (si, 0))
    qt, k, vt, gm, yr = pl.pallas_call(
        _proj_kernel,
        grid=(b, s // ts),
        in_specs=[
            seq_spec(D_MODEL), _const_spec((1, D_MODEL)), _const_spec((D_MODEL, Z_WIDTH)),
            _const_spec((1, Q_LORA_RANK)), _const_spec(wq.shape),
            _const_spec((1, KV_LORA_RANK)), _const_spec(wkv.shape),
            tab_spec(Q_PE_WIDTH), tab_spec(Q_PE_WIDTH), tab_spec(HEAD_PAD), tab_spec(HEAD_PAD),
            _const_spec((CONV_WIDTH, RNN_WIDTH)), _const_spec((1, RNN_WIDTH)),
            _const_spec(wgate.shape), _const_spec((2, RNN_WIDTH)),
            _const_spec((1, RNN_WIDTH)), _const_spec((1, RNN_WIDTH)),
        ],
        out_specs=[
            pl.BlockSpec((1, ts // tq, MLA_HEADS, HEAD_PAD, tq), lambda bi, si: (bi, si, 0, 0, 0)),
            pl.BlockSpec((1, MLA_HEADS, ts, HEAD_PAD), lambda bi, si: (bi, 0, si, 0)),
            pl.BlockSpec((1, ts // tq, MLA_HEADS, V_EXT, tq), lambda bi, si: (bi, si, 0, 0, 0)),
            seq_spec(MLA_WIDTH), seq_spec(RNN_WIDTH),
        ],
        out_shape=[
            jax.ShapeDtypeStruct((b, s // tq, MLA_HEADS, HEAD_PAD, tq), bf16),
            jax.ShapeDtypeStruct((b, MLA_HEADS, s, HEAD_PAD), bf16),
            jax.ShapeDtypeStruct((b, s // tq, MLA_HEADS, V_EXT, tq), bf16),
            jax.ShapeDtypeStruct((b, s, MLA_WIDTH), bf16),
            jax.ShapeDtypeStruct((b, s, RNN_WIDTH), bf16),
        ],
        scratch_shapes=[pltpu.VMEM((SUBLANES, RNN_WIDTH), f32),
                        pltpu.VMEM((1, RNN_WIDTH), f32)],
        compiler_params=params,
        name="proj_rnn",
    )(x, row(norm_in_g[0]), w_in_r, row(q_norm_g[0]), wq, row(kv_norm_g[0]), wkv,
      cq, sq, ck, sk, conv_w[0].astype(f32), row(conv_b[0]), wgate, bgate,
      row(lru_param[0]), row(out_norm_rnn_g[0]))

    yt = pl.pallas_call(
        _attn_kernel,
        grid=(b, s // tq),
        in_specs=[
            pl.BlockSpec((1, 1, MLA_HEADS, HEAD_PAD, tq), lambda bi, qi: (bi, qi, 0, 0, 0)),
            pl.BlockSpec((1, MLA_HEADS, s, HEAD_PAD), lambda bi, qi: (bi, 0, 0, 0)),
            pl.BlockSpec((1, s // tq, MLA_HEADS, V_EXT, tq), lambda bi, qi: (bi, 0, 0, 0, 0)),
        ],
        out_specs=pl.BlockSpec((1, MLA_WIDTH, tq), lambda bi, qi: (bi, 0, qi)),
        out_shape=jax.ShapeDtypeStruct((b, MLA_WIDTH, s), f32),
        scratch_shapes=[pltpu.VMEM((MLA_HEADS, tq, tq), f32),
                        pltpu.VMEM((MLA_HEADS, tq, tq), f32),
                        pltpu.VMEM((MLA_HEADS, tq, tq), bf16),
                        pltpu.VMEM((MLA_HEADS, tq, tq), bf16),
                        pltpu.VMEM((MLA_HEADS, 1, tq), f32),
                        pltpu.VMEM((MLA_HEADS, 1, tq), f32),
                        pltpu.VMEM((MLA_HEADS, 1, tq), f32),
                        pltpu.VMEM((MLA_HEADS, V_EXT, tq), f32)],
        compiler_params=params,
        name="mla_attention",
    )(qt, k, vt)

    return pl.pallas_call(
        _out_kernel,
        grid=(b, s // ts),
        in_specs=[
            pl.BlockSpec((1, MLA_WIDTH, ts), lambda bi, si: (bi, 0, si)),
            seq_spec(MLA_WIDTH), seq_spec(RNN_WIDTH), seq_spec(D_MODEL),
            _const_spec((1, MLA_WIDTH)), _const_spec((D_MODEL, D_MODEL)),
            _const_spec((1, D_MODEL)),
        ],
        out_specs=seq_spec(D_MODEL),
        out_shape=jax.ShapeDtypeStruct((b, s, D_MODEL), f32),
        compiler_params=params,
        name="out_proj",
    )(yt, gm, yr, x, row(out_norm_mla_g[0]), w_out[0].astype(bf16), row(final_norm_g))
```

```python
import math

import jax
import jax.numpy as jnp
from jax import lax
from jax.experimental import pallas as pl
from jax.experimental.pallas import tpu as pltpu

D_MODEL = 1024
CHUNK = 64
MLA_HEADS = 8
QK_NOPE_DIM = 64
QK_ROPE_DIM = 32
QK_HEAD_DIM = QK_NOPE_DIM + QK_ROPE_DIM
V_HEAD_DIM = 64
MLA_WIDTH = MLA_HEADS * V_HEAD_DIM
Q_LORA_RANK = 256
KV_LORA_RANK = 128
ROPE_THETA = 10000.0
RNN_WIDTH = 512
RNN_BLOCKS = 8
RNN_BLOCK_DIM = RNN_WIDTH // RNN_BLOCKS
CONV_WIDTH = 4
LRU_C = 8.0
NORM_EPS = 1e-6

LANES = 128
SUBLANES = 8
MXU_DIM = 256
HEAD_PAD = LANES
V_EXT = V_HEAD_DIM + 16
Z_WIDTH = 2048
SEQ_TILE = 512
ATT_BLOCK = MXU_DIM
VMEM_LIMIT = 56 * 1024 * 1024

Z_Q, Z_KV, Z_KPE, Z_GM, Z_XR, Z_GR = 0, 256, 384, 512, 1024, 1536
Q_NOPE, Q_PE, Q_ROT = 0, MLA_HEADS * QK_NOPE_DIM, MLA_HEADS * (QK_NOPE_DIM + QK_ROPE_DIM)
Q_PE_WIDTH = MLA_HEADS * QK_ROPE_DIM
GATE_GROUP = MXU_DIM

MASK_VALUE = -0.5 * float(jnp.finfo(jnp.float32).max)

f32 = jnp.float32
bf16 = jnp.bfloat16


def _rms(x, g):
    ms = jnp.mean(x * x, axis=-1, keepdims=True)
    return x * lax.rsqrt(ms + NORM_EPS) * g


def _sigmoid(x):
    return 0.5 * jnp.tanh(0.5 * x) + 0.5


def _silu(x):
    hx = 0.5 * x
    return hx * jnp.tanh(hx) + hx


def _shift_rows(x, tail, j):
    sh = pltpu.roll(x, j, 0)
    th = pltpu.roll(tail, j, 0)
    sub = lax.broadcasted_iota(jnp.int32, tail.shape, 0)
    head = jnp.where(sub < j, th, sh[:SUBLANES])
    return jnp.concatenate([head, sh[SUBLANES:]], axis=0)


def _proj_kernel(x_ref, gin_ref, win_ref, qg_ref, wq_ref, kvg_ref, wkv_ref,
                 cq_ref, sq_ref, ck_ref, sk_ref, convw_ref, convb_ref, wgate_ref,
                 bgate_ref, lru_ref, grnn_ref,
                 qt_out, k_out, vt_out, gm_out, yr_out,
                 xr_tail, h_carry):
    s = pl.program_id(1)
    ts = x_ref.shape[1]
    x = x_ref[0]
    h = _rms(x, gin_ref[...]).astype(bf16)
    z = jnp.dot(h, win_ref[...], preferred_element_type=f32)

    scale = math.log2(math.e) / math.sqrt(QK_HEAD_DIM)
    qn = _rms(z[:, Z_Q:Z_Q + Q_LORA_RANK], qg_ref[...]).astype(bf16)
    q = jnp.dot(qn, wq_ref[...], preferred_element_type=f32)
    q_nope_t = (q[:, Q_NOPE:Q_PE] * scale).T
    q_pe = q[:, Q_PE:Q_ROT] * cq_ref[...] + q[:, Q_ROT:] * sq_ref[...]
    q_pe_t = q_pe.T
    pad = jnp.zeros((HEAD_PAD - QK_HEAD_DIM, ATT_BLOCK), bf16)
    for j in range(ts // ATT_BLOCK):
        cols = slice(j * ATT_BLOCK, (j + 1) * ATT_BLOCK)
        for hh in range(MLA_HEADS):
            qt_out[0, j, hh, :QK_NOPE_DIM, :] = (
                q_nope_t[hh * QK_NOPE_DIM:(hh + 1) * QK_NOPE_DIM, cols].astype(bf16))
            qt_out[0, j, hh, QK_NOPE_DIM:QK_HEAD_DIM, :] = (
                q_pe_t[hh * QK_ROPE_DIM:(hh + 1) * QK_ROPE_DIM, cols].astype(bf16))
            qt_out[0, j, hh, QK_HEAD_DIM:, :] = pad

    kvn = _rms(z[:, Z_KV:Z_KV + KV_LORA_RANK], kvg_ref[...]).astype(bf16)
    kv = jnp.dot(kvn, wkv_ref[...], preferred_element_type=f32)
    kblk = z[:, Z_KPE:Z_KPE + HEAD_PAD]
    kpe = kblk * ck_ref[...] + pltpu.roll(kblk, LANES - QK_ROPE_DIM, 1) * sk_ref[...]
    for hh in range(MLA_HEADS):
        k_out[0, hh] = (kv[:, hh * HEAD_PAD:(hh + 1) * HEAD_PAD] + kpe).astype(bf16)
    vt = kv[:, MLA_HEADS * HEAD_PAD:].T
    ones_row = lax.broadcasted_iota(jnp.int32, (MLA_HEADS, V_EXT - V_HEAD_DIM, ATT_BLOCK), 1) == 0
    for j in range(ts // ATT_BLOCK):
        blk = vt[:, j * ATT_BLOCK:(j + 1) * ATT_BLOCK]
        vt_out[0, j, :, :V_HEAD_DIM, :] = (
            blk.reshape(MLA_HEADS, V_HEAD_DIM, ATT_BLOCK).astype(bf16))
        vt_out[0, j, :, V_HEAD_DIM:, :] = jnp.where(ones_row, 1.0, 0.0).astype(bf16)

    gm_out[0] = _silu(z[:, Z_GM:Z_GM + MLA_WIDTH]).astype(bf16)

    @pl.when(s == 0)
    def _():
        xr_tail[...] = jnp.zeros(xr_tail.shape, f32)
        h_carry[...] = jnp.zeros(h_carry.shape, f32)

    xin = z[:, Z_XR:Z_XR + RNN_WIDTH]
    tail = xr_tail[...]
    xr = convb_ref[...] + convw_ref[CONV_WIDTH - 1:CONV_WIDTH, :] * xin
    for j in range(1, CONV_WIDTH):
        k = CONV_WIDTH - 1 - j
        xr = xr + convw_ref[k:k + 1, :] * _shift_rows(xin, tail, j)
    xr_tail[...] = xin[ts - SUBLANES:, :]

    xb = xr.astype(bf16)
    ga, gx = [], []
    for g in range(RNN_WIDTH // GATE_GROUP):
        gg = jnp.dot(xb[:, g * GATE_GROUP:(g + 1) * GATE_GROUP], wgate_ref[g],
                     preferred_element_type=f32)
        ga.append(gg[:, :GATE_GROUP])
        gx.append(gg[:, GATE_GROUP:])
    r = _sigmoid(jnp.concatenate(ga, axis=1) + bgate_ref[0:1, :])
    i = _sigmoid(jnp.concatenate(gx, axis=1) + bgate_ref[1:2, :])
    nl = -lru_ref[...]
    softplus = jnp.maximum(nl, 0.0) + jnp.log1p(jnp.exp(-jnp.abs(nl)))
    a = jnp.exp2(r * ((-LRU_C * math.log2(math.e)) * softplus))
    m2 = jnp.maximum(1.0 - a * a, 1e-12)
    mult = m2 * lax.rsqrt(m2)
    row = lax.broadcasted_iota(jnp.int32, (ts, RNN_WIDTH), 0)
    mult = jnp.where((row == 0) & (s == 0), 1.0, mult)
    u = mult * (i * xr)

    groups = ts // SUBLANES
    a3 = a.reshape(groups, SUBLANES, RNN_WIDTH)
    u3 = u.reshape(groups, SUBLANES, RNN_WIDTH)
    sub = lax.broadcasted_iota(jnp.int32, (1, SUBLANES, RNN_WIDTH), 1)
    for d in (1, 2, 4):
        keep = sub >= d
        u3 = jnp.where(keep, a3 * pltpu.roll(u3, d, 1) + u3, u3)
        a3 = jnp.where(keep, a3 * pltpu.roll(a3, d, 1), a3)
    carry = h_carry[...]
    outs = []
    for g in range(groups):
        hg = u3[g] + a3[g] * carry
        outs.append(hg)
        carry = hg[SUBLANES - 1:SUBLANES, :]
    h_carry[...] = carry
    y = jnp.concatenate(outs, axis=0)

    yr_out[0] = (_rms(y, grnn_ref[...]) * _silu(z[:, Z_GR:Z_GR + RNN_WIDTH])).astype(bf16)


def _attn_kernel(qt_ref, k_ref, vt_ref, o_ref, s_even, s_odd, c_even, c_odd, m_ref, acc_ref):
    qi = pl.program_id(1)
    tq = ATT_BLOCK
    even = (s_even, c_even)
    odd = (s_odd, c_odd)

    m_ref[...] = jnp.full(m_ref.shape, MASK_VALUE, f32)
    acc_ref[...] = jnp.zeros(acc_ref.shape, f32)

    def qk(hh, j, dst, diagonal):
        s_dst, c_dst = dst
        kb = k_ref[0, hh, pl.ds(pl.multiple_of(j * tq, tq), tq), :]
        st = jnp.dot(kb, qt_ref[0, 0, hh], preferred_element_type=f32)
        if diagonal:
            kc = lax.broadcasted_iota(jnp.int32, st.shape, 0) // CHUNK
            qc = lax.broadcasted_iota(jnp.int32, st.shape, 1) // CHUNK
            st = jnp.where(kc <= qc, st, MASK_VALUE)
        s_dst[hh] = st
        c_dst[hh] = jnp.max(st, axis=0, keepdims=True)

    def attend(hh, j, src):
        s_src, c_src = src
        m_prev = m_ref[hh]
        m_new = jnp.maximum(m_prev, c_src[hh])
        alpha = jnp.exp2(m_prev - m_new)
        p = jnp.exp2(s_src[hh] - m_new).astype(bf16)
        upd = jnp.dot(vt_ref[0, j, hh], p, preferred_element_type=f32)
        acc_ref[hh] = alpha * acc_ref[hh] + upd
        m_ref[hh] = m_new

    def step(j, cur, nxt, next_is_diagonal):
        for hh in range(MLA_HEADS):
            qk(hh, j + 1, nxt, next_is_diagonal)
            attend(hh, j, cur)

    def finish(cur):
        for hh in range(MLA_HEADS):
            attend(hh, qi, cur)
            acc = acc_ref[hh]
            o_ref[0, hh * V_HEAD_DIM:(hh + 1) * V_HEAD_DIM, :] = (
                acc[:V_HEAD_DIM] / acc[V_HEAD_DIM:V_HEAD_DIM + 1])

    @pl.when(qi == 0)
    def _():
        for hh in range(MLA_HEADS):
            qk(hh, 0, even, True)
        finish(even)

    @pl.when(qi > 0)
    def _():
        for hh in range(MLA_HEADS):
            qk(hh, 0, even, False)

    def body(j, carry):
        @pl.when(lax.rem(j, 2) == 0)
        def _():
            step(j, even, odd, False)

        @pl.when(lax.rem(j, 2) == 1)
        def _():
            step(j, odd, even, False)
        return carry

    lax.fori_loop(0, qi - 1, body, 0)

    @pl.when((qi > 0) & (lax.rem(qi, 2) == 1))
    def _():
        step(qi - 1, even, odd, True)
        finish(odd)

    @pl.when((qi > 0) & (lax.rem(qi, 2) == 0))
    def _():
        step(qi - 1, odd, even, True)
        finish(even)


def _out_kernel(yt_ref, gm_ref, yr_ref, x_ref, gmla_ref, wout_ref, gfin_ref, o_ref):
    ym = yt_ref[0].T
    ym = _rms(ym, gmla_ref[...]) * gm_ref[0].astype(f32)
    y = jnp.concatenate([ym.astype(bf16), yr_ref[0]], axis=-1)
    xo = x_ref[0] + jnp.dot(y, wout_ref[...], preferred_element_type=f32)
    o_ref[0] = _rms(xo, gfin_ref[...])


def _rot_half(w):
    half = QK_ROPE_DIM // 2
    return jnp.concatenate([-w[..., half:], w[..., :half]], axis=-1)


def _prepare_weights(w_in, w_q_b, w_kv_b, w_rg_a, w_rg_x):
    offs = [0, 256, 384, 416, 928, 1440, 1952]
    w_qlat, w_kvlat, w_krope, w_gm, w_xr, w_gr = (
        w_in[:, offs[i]:offs[i + 1]] for i in range(6))
    kpe_blk = jnp.concatenate(
        [jnp.zeros((D_MODEL, QK_NOPE_DIM), w_in.dtype), w_krope, _rot_half(w_krope)], axis=1)
    w_in_r = jnp.concatenate([w_qlat, w_kvlat, kpe_blk, w_gm, w_xr, w_gr], axis=1)

    wq = w_q_b.reshape(Q_LORA_RANK, MLA_HEADS, QK_HEAD_DIM)
    wq_nope = wq[:, :, :QK_NOPE_DIM].reshape(Q_LORA_RANK, -1)
    wq_pe = wq[:, :, QK_NOPE_DIM:]
    wq_r = jnp.concatenate([wq_nope, wq_pe.reshape(Q_LORA_RANK, -1),
                            _rot_half(wq_pe).reshape(Q_LORA_RANK, -1)], axis=1)

    wkv = w_kv_b.reshape(KV_LORA_RANK, MLA_HEADS, QK_NOPE_DIM + V_HEAD_DIM)
    wk = jnp.pad(wkv[:, :, :QK_NOPE_DIM], ((0, 0), (0, 0), (0, HEAD_PAD - QK_NOPE_DIM)))
    wk = wk.reshape(KV_LORA_RANK, MLA_HEADS * HEAD_PAD)
    wv = wkv[:, :, QK_NOPE_DIM:].reshape(KV_LORA_RANK, MLA_WIDTH)
    wkv_r = jnp.concatenate([wk, wv], axis=1)

    per = GATE_GROUP // RNN_BLOCK_DIM
    eye = jnp.eye(per, dtype=w_rg_a.dtype)

    def block_diag(w):
        w = w.reshape(RNN_BLOCKS // per, per, RNN_BLOCK_DIM, RNN_BLOCK_DIM)
        return jnp.einsum('gncd,nm->gncmd', w, eye).reshape(-1, GATE_GROUP, GATE_GROUP)

    wgate = jnp.concatenate([block_diag(w_rg_a), block_diag(w_rg_x)], axis=2)
    return (w_in_r.astype(bf16), wq_r.astype(bf16), wkv_r.astype(bf16), wgate.astype(bf16))


def _rope_tables(seq):
    pos = jnp.arange(seq, dtype=f32)
    inv_freq = ROPE_THETA ** (-jnp.arange(0, QK_ROPE_DIM, 2, dtype=f32) / QK_ROPE_DIM)
    ang = pos[:, None] * inv_freq[None, :]
    cos = jnp.tile(jnp.cos(ang), (1, 2))
    sin = jnp.tile(jnp.sin(ang), (1, 2))
    scale = math.log2(math.e) / math.sqrt(QK_HEAD_DIM)
    cq = jnp.tile(cos, (1, MLA_HEADS)) * scale
    sq = jnp.tile(sin, (1, MLA_HEADS)) * scale
    pads = ((0, 0), (QK_NOPE_DIM, HEAD_PAD - QK_HEAD_DIM))
    return cq, sq, jnp.pad(cos, pads), jnp.pad(sin, pads)


def _const_spec(shape):
    return pl.BlockSpec(shape, lambda *_: (0,) * len(shape))


def kernel(x, norm_in_g, w_in, q_norm_g, w_q_b, kv_norm_g, w_kv_b, conv_w, conv_b,
           w_rg_a, b_rg_a, w_rg_x, b_rg_x, lru_param, out_norm_mla_g, out_norm_rnn_g,
           w_out, final_norm_g):
    b, s, _ = x.shape
    ts = SEQ_TILE
    tq = ATT_BLOCK
    assert s % ts == 0 and ts % tq == 0 and tq % CHUNK == 0
    assert w_in.shape[0] == 1, "the final norm is fused into the single layer's output kernel"
    cq, sq, ck, sk = _rope_tables(s)
    row = lambda v: v.reshape(1, -1).astype(f32)
    params = pltpu.CompilerParams(
        dimension_semantics=("arbitrary", "arbitrary"), vmem_limit_bytes=VMEM_LIMIT)

    w_in_r, wq, wkv, wgate = _prepare_weights(w_in[0], w_q_b[0], w_kv_b[0], w_rg_a[0], w_rg_x[0])
    bgate = jnp.stack([b_rg_a[0], b_rg_x[0]]).astype(f32)

    seq_spec = lambda w: pl.BlockSpec((1, ts, w), lambda bi, si: (bi, si, 0))
    tab_spec = lambda w: pl.BlockSpec((ts, w), lambda bi, si: (si, 0))
    qt, k, vt, gm, yr = pl.pallas_call(
        _proj_kernel,
        grid=(b, s // ts),
        in_specs=[
            seq_spec(D_MODEL), _const_spec((1, D_MODEL)), _const_spec((D_MODEL, Z_WIDTH)),
            _const_spec((1, Q_LORA_RANK)), _const_spec(wq.shape),
            _const_spec((1, KV_LORA_RANK)), _const_spec(wkv.shape),
            tab_spec(Q_PE_WIDTH), tab_spec(Q_PE_WIDTH), tab_spec(HEAD_PAD), tab_spec(HEAD_PAD),
            _const_spec((CONV_WIDTH, RNN_WIDTH)), _const_spec((1, RNN_WIDTH)),
            _const_spec(wgate.shape), _const_spec((2, RNN_WIDTH)),
            _const_spec((1, RNN_WIDTH)), _const_spec((1, RNN_WIDTH)),
        ],
        out_specs=[
            pl.BlockSpec((1, ts // tq, MLA_HEADS, HEAD_PAD, tq), lambda bi, si: (bi, si, 0, 0, 0)),
            pl.BlockSpec((1, MLA_HEADS, ts, HEAD_PAD), lambda bi, si: (bi, 0, si, 0)),
            pl.BlockSpec((1, ts // tq, MLA_HEADS, V_EXT, tq), lambda bi, si: (bi, si, 0, 0, 0)),
            seq_spec(MLA_WIDTH), seq_spec(RNN_WIDTH),
        ],
        out_shape=[
            jax.ShapeDtypeStruct((b, s // tq, MLA_HEADS, HEAD_PAD, tq), bf16),
            jax.ShapeDtypeStruct((b, MLA_HEADS, s, HEAD_PAD), bf16),
            jax.ShapeDtypeStruct((b, s // tq, MLA_HEADS, V_EXT, tq), bf16),
            jax.ShapeDtypeStruct((b, s, MLA_WIDTH), bf16),
            jax.ShapeDtypeStruct((b, s, RNN_WIDTH), bf16),
        ],
        scratch_shapes=[pltpu.VMEM((SUBLANES, RNN_WIDTH), f32),
                        pltpu.VMEM((1, RNN_WIDTH), f32)],
        compiler_params=params,
        name="proj_rnn",
    )(x, row(norm_in_g[0]), w_in_r, row(q_norm_g[0]), wq, row(kv_norm_g[0]), wkv,
      cq, sq, ck, sk, conv_w[0].astype(f32), row(conv_b[0]), wgate, bgate,
      row(lru_param[0]), row(out_norm_rnn_g[0]))

    yt = pl.pallas_call(
        _attn_kernel,
        grid=(b, s // tq),
        in_specs=[
            pl.BlockSpec((1, 1, MLA_HEADS, HEAD_PAD, tq), lambda bi, qi: (bi, qi, 0, 0, 0)),
            pl.BlockSpec((1, MLA_HEADS, s, HEAD_PAD), lambda bi, qi: (bi, 0, 0, 0)),
            pl.BlockSpec((1, s // tq, MLA_HEADS, V_EXT, tq), lambda bi, qi: (bi, 0, 0, 0, 0)),
        ],
        out_specs=pl.BlockSpec((1, MLA_WIDTH, tq), lambda bi, qi: (bi, 0, qi)),
        out_shape=jax.ShapeDtypeStruct((b, MLA_WIDTH, s), f32),
        scratch_shapes=[pltpu.VMEM((MLA_HEADS, tq, tq), f32),
                        pltpu.VMEM((MLA_HEADS, tq, tq), f32),
                        pltpu.VMEM((MLA_HEADS, 1, tq), f32),
                        pltpu.VMEM((MLA_HEADS, 1, tq), f32),
                        pltpu.VMEM((MLA_HEADS, 1, tq), f32),
                        pltpu.VMEM((MLA_HEADS, V_EXT, tq), f32)],
        compiler_params=params,
        name="mla_attention",
    )(qt, k, vt)

    return pl.pallas_call(
        _out_kernel,
        grid=(b, s // ts),
        in_specs=[
            pl.BlockSpec((1, MLA_WIDTH, ts), lambda bi, si: (bi, 0, si)),
            seq_spec(MLA_WIDTH), seq_spec(RNN_WIDTH), seq_spec(D_MODEL),
            _const_spec((1, MLA_WIDTH)), _const_spec((D_MODEL, D_MODEL)),
            _const_spec((1, D_MODEL)),
        ],
        out_specs=seq_spec(D_MODEL),
        out_shape=jax.ShapeDtypeStruct((b, s, D_MODEL), f32),
        compiler_params=params,
        name="out_proj",
    )(yt, gm, yr, x, row(out_norm_mla_g[0]), w_out[0].astype(bf16), row(final_norm_g))
```

```python
import math

import jax
import jax.numpy as jnp
from jax import lax
from jax.experimental import pallas as pl
from jax.experimental.pallas import tpu as pltpu

D_MODEL = 1024
CHUNK = 64
MLA_HEADS = 8
QK_NOPE_DIM = 64
QK_ROPE_DIM = 32
QK_HEAD_DIM = QK_NOPE_DIM + QK_ROPE_DIM
V_HEAD_DIM = 64
MLA_WIDTH = MLA_HEADS * V_HEAD_DIM
Q_LORA_RANK = 256
KV_LORA_RANK = 128
ROPE_THETA = 10000.0
RNN_WIDTH = 512
RNN_BLOCKS = 8
RNN_BLOCK_DIM = RNN_WIDTH // RNN_BLOCKS
CONV_WIDTH = 4
LRU_C = 8.0
NORM_EPS = 1e-6

LANES = 128
SUBLANES = 8
MXU_DIM = 256
HEAD_PAD = LANES
V_EXT = V_HEAD_DIM + 16
Z_WIDTH = 2048
SEQ_TILE = 512
ATT_BLOCK = MXU_DIM
VMEM_LIMIT = 56 * 1024 * 1024

Z_Q, Z_KV, Z_KPE, Z_GM, Z_XR, Z_GR = 0, 256, 384, 512, 1024, 1536
Q_NOPE, Q_PE, Q_ROT = 0, MLA_HEADS * QK_NOPE_DIM, MLA_HEADS * (QK_NOPE_DIM + QK_ROPE_DIM)
Q_PE_WIDTH = MLA_HEADS * QK_ROPE_DIM
GATE_GROUP = MXU_DIM
PERM_PITCH = SEQ_TILE // SUBLANES + SUBLANES

MASK_VALUE = -0.5 * float(jnp.finfo(jnp.float32).max)

f32 = jnp.float32
bf16 = jnp.bfloat16


def _rms(x, g):
    ms = jnp.mean(x * x, axis=-1, keepdims=True)
    return x * lax.rsqrt(ms + NORM_EPS) * g


def _sigmoid(x):
    return 0.5 * jnp.tanh(0.5 * x) + 0.5


def _silu(x):
    hx = 0.5 * x
    return hx * jnp.tanh(hx) + hx


def _to_lockstep(x, buf):
    rows = x.shape[0]
    seg = rows // SUBLANES
    slabs = x.shape[1] // LANES
    for c in range(slabs):
        for r in range(SUBLANES):
            buf[c, r * PERM_PITCH:r * PERM_PITCH + seg, :] = (
                x[r * seg:(r + 1) * seg, c * LANES:(c + 1) * LANES])
    return [jnp.concatenate([buf[c, pl.ds(i, SUBLANES, stride=PERM_PITCH), :]
                             for c in range(slabs)], axis=1) for i in range(seg)]


def _from_lockstep(pieces, buf):
    seg = len(pieces)
    slabs = pieces[0].shape[1] // LANES
    for i, piece in enumerate(pieces):
        for c in range(slabs):
            buf[c, pl.ds(i, SUBLANES, stride=PERM_PITCH), :] = piece[:, c * LANES:(c + 1) * LANES]
    return jnp.concatenate(
        [jnp.concatenate([buf[c, r * PERM_PITCH:r * PERM_PITCH + seg, :] for c in range(slabs)],
                         axis=1) for r in range(SUBLANES)], axis=0)


def _proj_kernel(x_ref, gin_ref, win_ref, qg_ref, wq_ref, kvg_ref, wkv_ref,
                 cq_ref, sq_ref, ck_ref, sk_ref, convw_ref, convb_ref, wgate_ref,
                 bgate_ref, lru_ref, grnn_ref,
                 qt_out, k_out, vt_out, gm_out, yr_out,
                 xr_tail, h_carry, perm_buf):
    s = pl.program_id(1)
    ts = x_ref.shape[1]
    x = x_ref[0]
    h = _rms(x, gin_ref[...]).astype(bf16)
    z = jnp.dot(h, win_ref[...], preferred_element_type=f32)

    scale = math.log2(math.e) / math.sqrt(QK_HEAD_DIM)
    qn = _rms(z[:, Z_Q:Z_Q + Q_LORA_RANK], qg_ref[...]).astype(bf16)
    q = jnp.dot(qn, wq_ref[...], preferred_element_type=f32)
    q_nope_t = (q[:, Q_NOPE:Q_PE] * scale).T
    q_pe = q[:, Q_PE:Q_ROT] * cq_ref[...] + q[:, Q_ROT:] * sq_ref[...]
    q_pe_t = q_pe.T
    pad = jnp.zeros((HEAD_PAD - QK_HEAD_DIM, ATT_BLOCK), bf16)
    for j in range(ts // ATT_BLOCK):
        cols = slice(j * ATT_BLOCK, (j + 1) * ATT_BLOCK)
        for hh in range(MLA_HEADS):
            qt_out[0, j, hh, :QK_NOPE_DIM, :] = (
                q_nope_t[hh * QK_NOPE_DIM:(hh + 1) * QK_NOPE_DIM, cols].astype(bf16))
            qt_out[0, j, hh, QK_NOPE_DIM:QK_HEAD_DIM, :] = (
                q_pe_t[hh * QK_ROPE_DIM:(hh + 1) * QK_ROPE_DIM, cols].astype(bf16))
            qt_out[0, j, hh, QK_HEAD_DIM:, :] = pad

    kvn = _rms(z[:, Z_KV:Z_KV + KV_LORA_RANK], kvg_ref[...]).astype(bf16)
    kv = jnp.dot(kvn, wkv_ref[...], preferred_element_type=f32)
    kblk = z[:, Z_KPE:Z_KPE + HEAD_PAD]
    kpe = kblk * ck_ref[...] + pltpu.roll(kblk, LANES - QK_ROPE_DIM, 1) * sk_ref[...]
    for hh in range(MLA_HEADS):
        k_out[0, hh] = (kv[:, hh * HEAD_PAD:(hh + 1) * HEAD_PAD] + kpe).astype(bf16)
    vt = kv[:, MLA_HEADS * HEAD_PAD:].T
    ones_row = lax.broadcasted_iota(jnp.int32, (MLA_HEADS, V_EXT - V_HEAD_DIM, ATT_BLOCK), 1) == 0
    for j in range(ts // ATT_BLOCK):
        blk = vt[:, j * ATT_BLOCK:(j + 1) * ATT_BLOCK]
        vt_out[0, j, :, :V_HEAD_DIM, :] = (
            blk.reshape(MLA_HEADS, V_HEAD_DIM, ATT_BLOCK).astype(bf16))
        vt_out[0, j, :, V_HEAD_DIM:, :] = jnp.where(ones_row, 1.0, 0.0).astype(bf16)

    gm_out[0] = _silu(z[:, Z_GM:Z_GM + MLA_WIDTH]).astype(bf16)

    @pl.when(s == 0)
    def _():
        xr_tail[...] = jnp.zeros(xr_tail.shape, f32)
        h_carry[...] = jnp.zeros(h_carry.shape, f32)

    sub8 = lax.broadcasted_iota(jnp.int32, (SUBLANES, RNN_WIDTH), 0)
    xl = _to_lockstep(z[:, Z_XR:Z_XR + RNN_WIDTH], perm_buf)
    hist = []
    for k in range(CONV_WIDTH - 1):
        prev = xr_tail[k]
        cur = xl[len(xl) - (CONV_WIDTH - 1) + k]
        hist.append(pltpu.roll(jnp.where(sub8 == SUBLANES - 1, prev, cur), 1, 0))
        xr_tail[k] = cur
    xext = jnp.concatenate(hist + xl, axis=0)
    xr = convb_ref[...]
    for k in range(CONV_WIDTH):
        xr = xr + convw_ref[k:k + 1, :] * xext[k * SUBLANES:k * SUBLANES + ts]

    xb = xr.astype(bf16)
    ga, gx = [], []
    for g in range(RNN_WIDTH // GATE_GROUP):
        gg = jnp.dot(xb[:, g * GATE_GROUP:(g + 1) * GATE_GROUP], wgate_ref[g],
                     preferred_element_type=f32)
        ga.append(gg[:, :GATE_GROUP])
        gx.append(gg[:, GATE_GROUP:])
    r = _sigmoid(jnp.concatenate(ga, axis=1) + bgate_ref[0:1, :])
    i = _sigmoid(jnp.concatenate(gx, axis=1) + bgate_ref[1:2, :])
    nl = -lru_ref[...]
    softplus = jnp.maximum(nl, 0.0) + jnp.log1p(jnp.exp(-jnp.abs(nl)))
    a = jnp.exp2(r * ((-LRU_C * math.log2(math.e)) * softplus))
    m2 = jnp.maximum(1.0 - a * a, 1e-12)
    mult = m2 * lax.rsqrt(m2)
    row = lax.broadcasted_iota(jnp.int32, (ts, RNN_WIDTH), 0)
    mult = jnp.where((row == 0) & (s == 0), 1.0, mult)
    u = mult * (i * xr)

    steps = ts // SUBLANES
    hl, pr = [], []
    hcur = pcur = None
    for t in range(steps):
        at = a[t * SUBLANES:(t + 1) * SUBLANES]
        ut = u[t * SUBLANES:(t + 1) * SUBLANES]
        hcur = ut if t == 0 else at * hcur + ut
        pcur = at if t == 0 else at * pcur
        hl.append(hcur)
        pr.append(pcur)
    pe, he = pcur, hcur
    for d in (1, 2, 4):
        keep = sub8 >= d
        he = jnp.where(keep, pe * pltpu.roll(he, d, 0) + he, he)
        pe = jnp.where(keep, pe * pltpu.roll(pe, d, 0), pe)
    carry = h_carry[...]
    ends = he + pe * carry
    h_carry[...] = ends[SUBLANES - 1:SUBLANES, :]
    init = jnp.where(sub8 == 0, carry, pltpu.roll(ends, 1, 0))
    y = _from_lockstep([hl[t] + pr[t] * init for t in range(steps)], perm_buf)

    yr_out[0] = (_rms(y, grnn_ref[...]) * _silu(z[:, Z_GR:Z_GR + RNN_WIDTH])).astype(bf16)


def _attn_kernel(qt_ref, k_ref, vt_ref, o_ref, s_even, s_odd, c_even, c_odd, m_ref, acc_ref):
    qi = pl.program_id(1)
    tq = ATT_BLOCK
    even = (s_even, c_even)
    odd = (s_odd, c_odd)

    m_ref[...] = jnp.full(m_ref.shape, MASK_VALUE, f32)
    acc_ref[...] = jnp.zeros(acc_ref.shape, f32)

    def qk(hh, j, dst, diagonal):
        s_dst, c_dst = dst
        kb = k_ref[0, hh, pl.ds(pl.multiple_of(j * tq, tq), tq), :]
        st = jnp.dot(kb, qt_ref[0, 0, hh], preferred_element_type=f32)
        if diagonal:
            kc = lax.broadcasted_iota(jnp.int32, st.shape, 0) // CHUNK
            qc = lax.broadcasted_iota(jnp.int32, st.shape, 1) // CHUNK
            st = jnp.where(kc <= qc, st, MASK_VALUE)
        s_dst[hh] = st
        c_dst[hh] = jnp.max(st, axis=0, keepdims=True)

    def attend(hh, j, src):
        s_src, c_src = src
        m_prev = m_ref[hh]
        m_new = jnp.maximum(m_prev, c_src[hh])
        alpha = jnp.exp2(m_prev - m_new)
        p = jnp.exp2(s_src[hh] - m_new).astype(bf16)
        upd = jnp.dot(vt_ref[0, j, hh], p, preferred_element_type=f32)
        acc_ref[hh] = alpha * acc_ref[hh] + upd
        m_ref[hh] = m_new

    def step(j, cur, nxt, next_is_diagonal):
        for hh in range(MLA_HEADS):
            qk(hh, j + 1, nxt, next_is_diagonal)
            attend(hh, j, cur)

    def finish(cur):
        for hh in range(MLA_HEADS):
            attend(hh, qi, cur)
            acc = acc_ref[hh]
            o_ref[0, hh * V_HEAD_DIM:(hh + 1) * V_HEAD_DIM, :] = (
                acc[:V_HEAD_DIM] / acc[V_HEAD_DIM:V_HEAD_DIM + 1])

    @pl.when(qi == 0)
    def _():
        for hh in range(MLA_HEADS):
            qk(hh, 0, even, True)
        finish(even)

    @pl.when(qi > 0)
    def _():
        for hh in range(MLA_HEADS):
            qk(hh, 0, even, False)

    def body(j, carry):
        @pl.when(lax.rem(j, 2) == 0)
        def _():
            step(j, even, odd, False)

        @pl.when(lax.rem(j, 2) == 1)
        def _():
            step(j, odd, even, False)
        return carry

    lax.fori_loop(0, qi - 1, body, 0)

    @pl.when((qi > 0) & (lax.rem(qi, 2) == 1))
    def _():
        step(qi - 1, even, odd, True)
        finish(odd)

    @pl.when((qi > 0) & (lax.rem(qi, 2) == 0))
    def _():
        step(qi - 1, odd, even, True)
        finish(even)


def _out_kernel(yt_ref, gm_ref, yr_ref, x_ref, gmla_ref, wout_ref, gfin_ref, o_ref):
    ym = yt_ref[0].T
    ym = _rms(ym, gmla_ref[...]) * gm_ref[0].astype(f32)
    y = jnp.concatenate([ym.astype(bf16), yr_ref[0]], axis=-1)
    xo = x_ref[0] + jnp.dot(y, wout_ref[...], preferred_element_type=f32)
    o_ref[0] = _rms(xo, gfin_ref[...])


def _rot_half(w):
    half = QK_ROPE_DIM // 2
    return jnp.concatenate([-w[..., half:], w[..., :half]], axis=-1)


def _prepare_weights(w_in, w_q_b, w_kv_b, w_rg_a, w_rg_x):
    offs = [0, 256, 384, 416, 928, 1440, 1952]
    w_qlat, w_kvlat, w_krope, w_gm, w_xr, w_gr = (
        w_in[:, offs[i]:offs[i + 1]] for i in range(6))
    kpe_blk = jnp.concatenate(
        [jnp.zeros((D_MODEL, QK_NOPE_DIM), w_in.dtype), w_krope, _rot_half(w_krope)], axis=1)
    w_in_r = jnp.concatenate([w_qlat, w_kvlat, kpe_blk, w_gm, w_xr, w_gr], axis=1)

    wq = w_q_b.reshape(Q_LORA_RANK, MLA_HEADS, QK_HEAD_DIM)
    wq_nope = wq[:, :, :QK_NOPE_DIM].reshape(Q_LORA_RANK, -1)
    wq_pe = wq[:, :, QK_NOPE_DIM:]
    wq_r = jnp.concatenate([wq_nope, wq_pe.reshape(Q_LORA_RANK, -1),
                            _rot_half(wq_pe).reshape(Q_LORA_RANK, -1)], axis=1)

    wkv = w_kv_b.reshape(KV_LORA_RANK, MLA_HEADS, QK_NOPE_DIM + V_HEAD_DIM)
    wk = jnp.pad(wkv[:, :, :QK_NOPE_DIM], ((0, 0), (0, 0), (0, HEAD_PAD - QK_NOPE_DIM)))
    wk = wk.reshape(KV_LORA_RANK, MLA_HEADS * HEAD_PAD)
    wv = wkv[:, :, QK_NOPE_DIM:].reshape(KV_LORA_RANK, MLA_WIDTH)
    wkv_r = jnp.concatenate([wk, wv], axis=1)

    per = GATE_GROUP // RNN_BLOCK_DIM
    eye = jnp.eye(per, dtype=w_rg_a.dtype)

    def block_diag(w):
        w = w.reshape(RNN_BLOCKS // per, per, RNN_BLOCK_DIM, RNN_BLOCK_DIM)
        return jnp.einsum('gncd,nm->gncmd', w, eye).reshape(-1, GATE_GROUP, GATE_GROUP)

    wgate = jnp.concatenate([block_diag(w_rg_a), block_diag(w_rg_x)], axis=2)
    return (w_in_r.astype(bf16), wq_r.astype(bf16), wkv_r.astype(bf16), wgate.astype(bf16))


def _rope_tables(seq):
    pos = jnp.arange(seq, dtype=f32)
    inv_freq = ROPE_THETA ** (-jnp.arange(0, QK_ROPE_DIM, 2, dtype=f32) / QK_ROPE_DIM)
    ang = pos[:, None] * inv_freq[None, :]
    cos = jnp.tile(jnp.cos(ang), (1, 2))
    sin = jnp.tile(jnp.sin(ang), (1, 2))
    scale = math.log2(math.e) / math.sqrt(QK_HEAD_DIM)
    cq = jnp.tile(cos, (1, MLA_HEADS)) * scale
    sq = jnp.tile(sin, (1, MLA_HEADS)) * scale
    pads = ((0, 0), (QK_NOPE_DIM, HEAD_PAD - QK_HEAD_DIM))
    return cq, sq, jnp.pad(cos, pads), jnp.pad(sin, pads)


def _const_spec(shape):
    return pl.BlockSpec(shape, lambda *_: (0,) * len(shape))


def kernel(x, norm_in_g, w_in, q_norm_g, w_q_b, kv_norm_g, w_kv_b, conv_w, conv_b,
           w_rg_a, b_rg_a, w_rg_x, b_rg_x, lru_param, out_norm_mla_g, out_norm_rnn_g,
           w_out, final_norm_g):
    b, s, _ = x.shape
    ts = SEQ_TILE
    tq = ATT_BLOCK
    assert s % ts == 0 and ts % tq == 0 and tq % CHUNK == 0
    assert w_in.shape[0] == 1, "the final norm is fused into the single layer's output kernel"
    cq, sq, ck, sk = _rope_tables(s)
    row = lambda v: v.reshape(1, -1).astype(f32)
    params = pltpu.CompilerParams(
        dimension_semantics=("arbitrary", "arbitrary"), vmem_limit_bytes=VMEM_LIMIT)

    w_in_r, wq, wkv, wgate = _prepare_weights(w_in[0], w_q_b[0], w_kv_b[0], w_rg_a[0], w_rg_x[0])
    bgate = jnp.stack([b_rg_a[0], b_rg_x[0]]).astype(f32)

    seq_spec = lambda w: pl.BlockSpec((1, ts, w), lambda bi, si: (bi, si, 0))
    tab_spec = lambda w: pl.BlockSpec((ts, w), lambda bi, si: (si, 0))
    qt, k, vt, gm, yr = pl.pallas_call(
        _proj_kernel,
        grid=(b, s // ts),
        in_specs=[
            seq_spec(D_MODEL), _const_spec((1, D_MODEL)), _const_spec((D_MODEL, Z_WIDTH)),
            _const_spec((1, Q_LORA_RANK)), _const_spec(wq.shape),
            _const_spec((1, KV_LORA_RANK)), _const_spec(wkv.shape),
            tab_spec(Q_PE_WIDTH), tab_spec(Q_PE_WIDTH), tab_spec(HEAD_PAD), tab_spec(HEAD_PAD),
            _const_spec((CONV_WIDTH, RNN_WIDTH)), _const_spec((1, RNN_WIDTH)),
            _const_spec(wgate.shape), _const_spec((2, RNN_WIDTH)),
            _const_spec((1, RNN_WIDTH)), _const_spec((1, RNN_WIDTH)),
        ],
        out_specs=[
            pl.BlockSpec((1, ts // tq, MLA_HEADS, HEAD_PAD, tq), lambda bi, si: (bi, si, 0, 0, 0)),
            pl.BlockSpec((1, MLA_HEADS, ts, HEAD_PAD), lambda bi, si: (bi, 0, si, 0)),
            pl.BlockSpec((1, ts // tq, MLA_HEADS, V_EXT, tq), lambda bi, si: (bi, si, 0, 0, 0)),
            seq_spec(MLA_WIDTH), seq_spec(RNN_WIDTH),
        ],
        out_shape=[
            jax.ShapeDtypeStruct((b, s // tq, MLA_HEADS, HEAD_PAD, tq), bf16),
            jax.ShapeDtypeStruct((b, MLA_HEADS, s, HEAD_PAD), bf16),
            jax.ShapeDtypeStruct((b, s // tq, MLA_HEADS, V_EXT, tq), bf16),
            jax.ShapeDtypeStruct((b, s, MLA_WIDTH), bf16),
            jax.ShapeDtypeStruct((b, s, RNN_WIDTH), bf16),
        ],
        scratch_shapes=[pltpu.VMEM((CONV_WIDTH - 1, SUBLANES, RNN_WIDTH), f32),
                        pltpu.VMEM((1, RNN_WIDTH), f32),
                        pltpu.VMEM((RNN_WIDTH // LANES, SUBLANES * PERM_PITCH, LANES), f32)],
        compiler_params=params,
        name="proj_rnn",
    )(x, row(norm_in_g[0]), w_in_r, row(q_norm_g[0]), wq, row(kv_norm_g[0]), wkv,
      cq, sq, ck, sk, conv_w[0].astype(f32), row(conv_b[0]), wgate, bgate,
      row(lru_param[0]), row(out_norm_rnn_g[0]))

    yt = pl.pallas_call(
        _attn_kernel,
        grid=(b, s // tq),
        in_specs=[
            pl.BlockSpec((1, 1, MLA_HEADS, HEAD_PAD, tq), lambda bi, qi: (bi, qi, 0, 0, 0)),
            pl.BlockSpec((1, MLA_HEADS, s, HEAD_PAD), lambda bi, qi: (bi, 0, 0, 0)),
            pl.BlockSpec((1, s // tq, MLA_HEADS, V_EXT, tq), lambda bi, qi: (bi, 0, 0, 0, 0)),
        ],
        out_specs=pl.BlockSpec((1, MLA_WIDTH, tq), lambda bi, qi: (bi, 0, qi)),
        out_shape=jax.ShapeDtypeStruct((b, MLA_WIDTH, s), f32),
        scratch_shapes=[pltpu.VMEM((MLA_HEADS, tq, tq), f32),
                        pltpu.VMEM((MLA_HEADS, tq, tq), f32),
                        pltpu.VMEM((MLA_HEADS, 1, tq), f32),
                        pltpu.VMEM((MLA_HEADS, 1, tq), f32),
                        pltpu.VMEM((MLA_HEADS, 1, tq), f32),
                        pltpu.VMEM((MLA_HEADS, V_EXT, tq), f32)],
        compiler_params=params,
        name="mla_attention",
    )(qt, k, vt)

    return pl.pallas_call(
        _out_kernel,
        grid=(b, s // ts),
        in_specs=[
            pl.BlockSpec((1, MLA_WIDTH, ts), lambda bi, si: (bi, 0, si)),
            seq_spec(MLA_WIDTH), seq_spec(RNN_WIDTH), seq_spec(D_MODEL),
            _const_spec((1, MLA_WIDTH)), _const_spec((D_MODEL, D_MODEL)),
            _const_spec((1, D_MODEL)),
        ],
        out_specs=seq_spec(D_MODEL),
        out_shape=jax.ShapeDtypeStruct((b, s, D_MODEL), f32),
        compiler_params=params,
        name="out_proj",
    )(yt, gm, yr, x, row(out_norm_mla_g[0]), w_out[0].astype(bf16), row(final_norm_g))
```

```python
import math

import jax
import jax.numpy as jnp
from jax import lax
from jax.experimental import pallas as pl
from jax.experimental.pallas import tpu as pltpu

D_MODEL = 1024
CHUNK = 64
MLA_HEADS = 8
QK_NOPE_DIM = 64
QK_ROPE_DIM = 32
QK_HEAD_DIM = QK_NOPE_DIM + QK_ROPE_DIM
V_HEAD_DIM = 64
MLA_WIDTH = MLA_HEADS * V_HEAD_DIM
Q_LORA_RANK = 256
KV_LORA_RANK = 128
ROPE_THETA = 10000.0
RNN_WIDTH = 512
RNN_BLOCKS = 8
RNN_BLOCK_DIM = RNN_WIDTH // RNN_BLOCKS
CONV_WIDTH = 4
LRU_C = 8.0
NORM_EPS = 1e-6

LANES = 128
SUBLANES = 8
MXU_DIM = 256
HEAD_PAD = LANES
V_EXT = V_HEAD_DIM + 16
Z_WIDTH = 2048
SEQ_TILE = 512
ATT_BLOCK = MXU_DIM
VMEM_LIMIT = 56 * 1024 * 1024

Z_Q, Z_KV, Z_KPE, Z_GM, Z_XR, Z_GR = 0, 256, 384, 512, 1024, 1536
Q_NOPE, Q_PE, Q_ROT = 0, MLA_HEADS * QK_NOPE_DIM, MLA_HEADS * (QK_NOPE_DIM + QK_ROPE_DIM)
Q_PE_WIDTH = MLA_HEADS * QK_ROPE_DIM
GATE_GROUP = MXU_DIM
PERM_PITCH = SEQ_TILE // SUBLANES + SUBLANES

MASK_VALUE = -0.5 * float(jnp.finfo(jnp.float32).max)

f32 = jnp.float32
bf16 = jnp.bfloat16


def _rms(x, g):
    ms = jnp.mean(x * x, axis=-1, keepdims=True)
    return x * lax.rsqrt(ms + NORM_EPS) * g


def _sigmoid(x):
    return 0.5 * jnp.tanh(0.5 * x) + 0.5


def _silu(x):
    hx = 0.5 * x
    return hx * jnp.tanh(hx) + hx


def _to_lockstep(x, buf):
    rows = x.shape[0]
    seg = rows // SUBLANES
    slabs = x.shape[1] // LANES
    for c in range(slabs):
        for r in range(SUBLANES):
            buf[c, r * PERM_PITCH:r * PERM_PITCH + seg, :] = (
                x[r * seg:(r + 1) * seg, c * LANES:(c + 1) * LANES])
    return [jnp.concatenate([buf[c, pl.ds(i, SUBLANES, stride=PERM_PITCH), :]
                             for c in range(slabs)], axis=1) for i in range(seg)]


def _from_lockstep(pieces, buf):
    seg = len(pieces)
    slabs = pieces[0].shape[1] // LANES
    for i, piece in enumerate(pieces):
        for c in range(slabs):
            buf[c, pl.ds(i, SUBLANES, stride=PERM_PITCH), :] = piece[:, c * LANES:(c + 1) * LANES]
    return jnp.concatenate(
        [jnp.concatenate([buf[c, r * PERM_PITCH:r * PERM_PITCH + seg, :] for c in range(slabs)],
                         axis=1) for r in range(SUBLANES)], axis=0)


def _proj_kernel(x_ref, gin_ref, win_ref, qg_ref, wq_ref, kvg_ref, wkv_ref,
                 cq_ref, sq_ref, ck_ref, sk_ref, convw_ref, convb_ref, wgate_ref,
                 bgate_ref, lru_ref, grnn_ref,
                 qt_out, k_out, vt_out, gm_out, yr_out,
                 xr_tail, h_carry, perm_buf):
    s = pl.program_id(1)
    ts = x_ref.shape[1]

    @pl.when(s == 0)
    def _():
        xr_tail[...] = jnp.zeros(xr_tail.shape, f32)
        h_carry[...] = jnp.zeros(h_carry.shape, f32)

    h = _rms(x_ref[0], gin_ref[...]).astype(bf16)

    def in_proj(lo, width):
        return jnp.dot(h, win_ref[:, lo:lo + width], preferred_element_type=f32)

    z_xr = in_proj(Z_XR, RNN_WIDTH)
    z_a = in_proj(0, Z_GM)
    xb, xr = _conv_lockstep(z_xr, xr_tail, perm_buf, convw_ref, convb_ref)
    gates = [jnp.dot(xb[:, g * GATE_GROUP:(g + 1) * GATE_GROUP], wgate_ref[g],
                     preferred_element_type=f32)
             for g in range(RNN_WIDTH // GATE_GROUP)]
    z_gm = in_proj(Z_GM, MLA_WIDTH)
    qn = _rms(z_a[:, Z_Q:Z_Q + Q_LORA_RANK], qg_ref[...]).astype(bf16)
    q = jnp.dot(qn, wq_ref[...], preferred_element_type=f32)
    kvn = _rms(z_a[:, Z_KV:Z_KV + KV_LORA_RANK], kvg_ref[...]).astype(bf16)
    kv = jnp.dot(kvn, wkv_ref[...], preferred_element_type=f32)
    z_gr = in_proj(Z_GR, RNN_WIDTH)

    y = _rg_lru(s, gates, xr, bgate_ref, lru_ref, h_carry, perm_buf)
    yr_out[0] = (_rms(y, grnn_ref[...]) * _silu(z_gr)).astype(bf16)

    scale = math.log2(math.e) / math.sqrt(QK_HEAD_DIM)
    q_nope_t = (q[:, Q_NOPE:Q_PE] * scale).T
    q_pe = q[:, Q_PE:Q_ROT] * cq_ref[...] + q[:, Q_ROT:] * sq_ref[...]
    q_pe_t = q_pe.T
    pad = jnp.zeros((HEAD_PAD - QK_HEAD_DIM, ATT_BLOCK), bf16)
    for j in range(ts // ATT_BLOCK):
        cols = slice(j * ATT_BLOCK, (j + 1) * ATT_BLOCK)
        for hh in range(MLA_HEADS):
            qt_out[0, j, hh, :QK_NOPE_DIM, :] = (
                q_nope_t[hh * QK_NOPE_DIM:(hh + 1) * QK_NOPE_DIM, cols].astype(bf16))
            qt_out[0, j, hh, QK_NOPE_DIM:QK_HEAD_DIM, :] = (
                q_pe_t[hh * QK_ROPE_DIM:(hh + 1) * QK_ROPE_DIM, cols].astype(bf16))
            qt_out[0, j, hh, QK_HEAD_DIM:, :] = pad

    kblk = z_a[:, Z_KPE:Z_KPE + HEAD_PAD]
    kpe = kblk * ck_ref[...] + pltpu.roll(kblk, LANES - QK_ROPE_DIM, 1) * sk_ref[...]
    for hh in range(MLA_HEADS):
        k_out[0, hh] = (kv[:, hh * HEAD_PAD:(hh + 1) * HEAD_PAD] + kpe).astype(bf16)
    vt = kv[:, MLA_HEADS * HEAD_PAD:].T
    ones_row = lax.broadcasted_iota(jnp.int32, (MLA_HEADS, V_EXT - V_HEAD_DIM, ATT_BLOCK), 1) == 0
    for j in range(ts // ATT_BLOCK):
        blk = vt[:, j * ATT_BLOCK:(j + 1) * ATT_BLOCK]
        vt_out[0, j, :, :V_HEAD_DIM, :] = (
            blk.reshape(MLA_HEADS, V_HEAD_DIM, ATT_BLOCK).astype(bf16))
        vt_out[0, j, :, V_HEAD_DIM:, :] = jnp.where(ones_row, 1.0, 0.0).astype(bf16)

    gm_out[0] = _silu(z_gm).astype(bf16)


def _conv_lockstep(x_rnn, xr_tail, perm_buf, convw_ref, convb_ref):
    ts = x_rnn.shape[0]
    sub8 = lax.broadcasted_iota(jnp.int32, (SUBLANES, RNN_WIDTH), 0)
    xl = _to_lockstep(x_rnn, perm_buf)
    hist = []
    for k in range(CONV_WIDTH - 1):
        prev = xr_tail[k]
        cur = xl[len(xl) - (CONV_WIDTH - 1) + k]
        hist.append(pltpu.roll(jnp.where(sub8 == SUBLANES - 1, prev, cur), 1, 0))
        xr_tail[k] = cur
    xext = jnp.concatenate(hist + xl, axis=0)
    xr = convb_ref[...]
    for k in range(CONV_WIDTH):
        xr = xr + convw_ref[k:k + 1, :] * xext[k * SUBLANES:k * SUBLANES + ts]
    return xr.astype(bf16), xr


def _rg_lru(s, gates, xr, bgate_ref, lru_ref, h_carry, perm_buf):
    ts = xr.shape[0]
    sub8 = lax.broadcasted_iota(jnp.int32, (SUBLANES, RNN_WIDTH), 0)
    ga = [gg[:, :GATE_GROUP] for gg in gates]
    gx = [gg[:, GATE_GROUP:] for gg in gates]
    r = _sigmoid(jnp.concatenate(ga, axis=1) + bgate_ref[0:1, :])
    i = _sigmoid(jnp.concatenate(gx, axis=1) + bgate_ref[1:2, :])
    nl = -lru_ref[...]
    softplus = jnp.maximum(nl, 0.0) + jnp.log1p(jnp.exp(-jnp.abs(nl)))
    a = jnp.exp2(r * ((-LRU_C * math.log2(math.e)) * softplus))
    m2 = jnp.maximum(1.0 - a * a, 1e-12)
    mult = m2 * lax.rsqrt(m2)
    row = lax.broadcasted_iota(jnp.int32, (ts, RNN_WIDTH), 0)
    mult = jnp.where((row == 0) & (s == 0), 1.0, mult)
    u = mult * (i * xr)

    steps = ts // SUBLANES
    hl, pr = [], []
    hcur = pcur = None
    for t in range(steps):
        at = a[t * SUBLANES:(t + 1) * SUBLANES]
        ut = u[t * SUBLANES:(t + 1) * SUBLANES]
        hcur = ut if t == 0 else at * hcur + ut
        pcur = at if t == 0 else at * pcur
        hl.append(hcur)
        pr.append(pcur)
    pe, he = pcur, hcur
    for d in (1, 2, 4):
        keep = sub8 >= d
        he = jnp.where(keep, pe * pltpu.roll(he, d, 0) + he, he)
        pe = jnp.where(keep, pe * pltpu.roll(pe, d, 0), pe)
    carry = h_carry[...]
    ends = he + pe * carry
    h_carry[...] = ends[SUBLANES - 1:SUBLANES, :]
    init = jnp.where(sub8 == 0, carry, pltpu.roll(ends, 1, 0))
    return _from_lockstep([hl[t] + pr[t] * init for t in range(steps)], perm_buf)


def _attn_kernel(qt_ref, k_ref, vt_ref, o_ref, s_even, s_odd, c_even, c_odd, m_ref, acc_ref):
    qi = pl.program_id(1)
    tq = ATT_BLOCK
    even = (s_even, c_even)
    odd = (s_odd, c_odd)

    m_ref[...] = jnp.full(m_ref.shape, MASK_VALUE, f32)
    acc_ref[...] = jnp.zeros(acc_ref.shape, f32)

    def qk(hh, j, dst, diagonal):
        s_dst, c_dst = dst
        kb = k_ref[0, hh, pl.ds(pl.multiple_of(j * tq, tq), tq), :]
        st = jnp.dot(kb, qt_ref[0, 0, hh], preferred_element_type=f32)
        if diagonal:
            kc = lax.broadcasted_iota(jnp.int32, st.shape, 0) // CHUNK
            qc = lax.broadcasted_iota(jnp.int32, st.shape, 1) // CHUNK
            st = jnp.where(kc <= qc, st, MASK_VALUE)
        s_dst[hh] = st
        c_dst[hh] = jnp.max(st, axis=0, keepdims=True)

    def attend(hh, j, src):
        s_src, c_src = src
        m_prev = m_ref[hh]
        m_new = jnp.maximum(m_prev, c_src[hh])
        alpha = jnp.exp2(m_prev - m_new)
        p = jnp.exp2(s_src[hh] - m_new).astype(bf16)
        upd = jnp.dot(vt_ref[0, j, hh], p, preferred_element_type=f32)
        acc_ref[hh] = alpha * acc_ref[hh] + upd
        m_ref[hh] = m_new

    def step(j, cur, nxt, next_is_diagonal):
        for hh in range(MLA_HEADS):
            qk(hh, j + 1, nxt, next_is_diagonal)
            attend(hh, j, cur)

    def finish(cur):
        for hh in range(MLA_HEADS):
            attend(hh, qi, cur)
            acc = acc_ref[hh]
            o_ref[0, hh * V_HEAD_DIM:(hh + 1) * V_HEAD_DIM, :] = (
                acc[:V_HEAD_DIM] / acc[V_HEAD_DIM:V_HEAD_DIM + 1])

    @pl.when(qi == 0)
    def _():
        for hh in range(MLA_HEADS):
            qk(hh, 0, even, True)
        finish(even)

    @pl.when(qi > 0)
    def _():
        for hh in range(MLA_HEADS):
            qk(hh, 0, even, False)

    def body(j, carry):
        @pl.when(lax.rem(j, 2) == 0)
        def _():
            step(j, even, odd, False)

        @pl.when(lax.rem(j, 2) == 1)
        def _():
            step(j, odd, even, False)
        return carry

    lax.fori_loop(0, qi - 1, body, 0)

    @pl.when((qi > 0) & (lax.rem(qi, 2) == 1))
    def _():
        step(qi - 1, even, odd, True)
        finish(odd)

    @pl.when((qi > 0) & (lax.rem(qi, 2) == 0))
    def _():
        step(qi - 1, odd, even, True)
        finish(even)


def _out_kernel(yt_ref, gm_ref, yr_ref, x_ref, gmla_ref, wout_ref, gfin_ref, o_ref):
    ym = yt_ref[0].T
    ym = _rms(ym, gmla_ref[...]) * gm_ref[0].astype(f32)
    y = jnp.concatenate([ym.astype(bf16), yr_ref[0]], axis=-1)
    xo = x_ref[0] + jnp.dot(y, wout_ref[...], preferred_element_type=f32)
    o_ref[0] = _rms(xo, gfin_ref[...])


def _rot_half(w):
    half = QK_ROPE_DIM // 2
    return jnp.concatenate([-w[..., half:], w[..., :half]], axis=-1)


def _prepare_weights(w_in, w_q_b, w_kv_b, w_rg_a, w_rg_x):
    offs = [0, 256, 384, 416, 928, 1440, 1952]
    w_qlat, w_kvlat, w_krope, w_gm, w_xr, w_gr = (
        w_in[:, offs[i]:offs[i + 1]] for i in range(6))
    kpe_blk = jnp.concatenate(
        [jnp.zeros((D_MODEL, QK_NOPE_DIM), w_in.dtype), w_krope, _rot_half(w_krope)], axis=1)
    w_in_r = jnp.concatenate([w_qlat, w_kvlat, kpe_blk, w_gm, w_xr, w_gr], axis=1)

    wq = w_q_b.reshape(Q_LORA_RANK, MLA_HEADS, QK_HEAD_DIM)
    wq_nope = wq[:, :, :QK_NOPE_DIM].reshape(Q_LORA_RANK, -1)
    wq_pe = wq[:, :, QK_NOPE_DIM:]
    wq_r = jnp.concatenate([wq_nope, wq_pe.reshape(Q_LORA_RANK, -1),
                            _rot_half(wq_pe).reshape(Q_LORA_RANK, -1)], axis=1)

    wkv = w_kv_b.reshape(KV_LORA_RANK, MLA_HEADS, QK_NOPE_DIM + V_HEAD_DIM)
    wk = jnp.pad(wkv[:, :, :QK_NOPE_DIM], ((0, 0), (0, 0), (0, HEAD_PAD - QK_NOPE_DIM)))
    wk = wk.reshape(KV_LORA_RANK, MLA_HEADS * HEAD_PAD)
    wv = wkv[:, :, QK_NOPE_DIM:].reshape(KV_LORA_RANK, MLA_WIDTH)
    wkv_r = jnp.concatenate([wk, wv], axis=1)

    per = GATE_GROUP // RNN_BLOCK_DIM
    eye = jnp.eye(per, dtype=w_rg_a.dtype)

    def block_diag(w):
        w = w.reshape(RNN_BLOCKS // per, per, RNN_BLOCK_DIM, RNN_BLOCK_DIM)
        return jnp.einsum('gncd,nm->gncmd', w, eye).reshape(-1, GATE_GROUP, GATE_GROUP)

    wgate = jnp.concatenate([block_diag(w_rg_a), block_diag(w_rg_x)], axis=2)
    return (w_in_r.astype(bf16), wq_r.astype(bf16), wkv_r.astype(bf16), wgate.astype(bf16))


def _rope_tables(seq):
    pos = jnp.arange(seq, dtype=f32)
    inv_freq = ROPE_THETA ** (-jnp.arange(0, QK_ROPE_DIM, 2, dtype=f32) / QK_ROPE_DIM)
    ang = pos[:, None] * inv_freq[None, :]
    cos = jnp.tile(jnp.cos(ang), (1, 2))
    sin = jnp.tile(jnp.sin(ang), (1, 2))
    scale = math.log2(math.e) / math.sqrt(QK_HEAD_DIM)
    cq = jnp.tile(cos, (1, MLA_HEADS)) * scale
    sq = jnp.tile(sin, (1, MLA_HEADS)) * scale
    pads = ((0, 0), (QK_NOPE_DIM, HEAD_PAD - QK_HEAD_DIM))
    return cq, sq, jnp.pad(cos, pads), jnp.pad(sin, pads)


def _const_spec(shape):
    return pl.BlockSpec(shape, lambda *_: (0,) * len(shape))


def kernel(x, norm_in_g, w_in, q_norm_g, w_q_b, kv_norm_g, w_kv_b, conv_w, conv_b,
           w_rg_a, b_rg_a, w_rg_x, b_rg_x, lru_param, out_norm_mla_g, out_norm_rnn_g,
           w_out, final_norm_g):
    b, s, _ = x.shape
    ts = SEQ_TILE
    tq = ATT_BLOCK
    assert s % ts == 0 and ts % tq == 0 and tq % CHUNK == 0
    assert w_in.shape[0] == 1, "the final norm is fused into the single layer's output kernel"
    cq, sq, ck, sk = _rope_tables(s)
    row = lambda v: v.reshape(1, -1).astype(f32)
    params = pltpu.CompilerParams(
        dimension_semantics=("arbitrary", "arbitrary"), vmem_limit_bytes=VMEM_LIMIT)

    w_in_r, wq, wkv, wgate = _prepare_weights(w_in[0], w_q_b[0], w_kv_b[0], w_rg_a[0], w_rg_x[0])
    bgate = jnp.stack([b_rg_a[0], b_rg_x[0]]).astype(f32)

    seq_spec = lambda w: pl.BlockSpec((1, ts, w), lambda bi, si: (bi, si, 0))
    tab_spec = lambda w: pl.BlockSpec((ts, w), lambda bi, si: (si, 0))
    qt, k, vt, gm, yr = pl.pallas_call(
        _proj_kernel,
        grid=(b, s // ts),
        in_specs=[
            seq_spec(D_MODEL), _const_spec((1, D_MODEL)), _const_spec((D_MODEL, Z_WIDTH)),
            _const_spec((1, Q_LORA_RANK)), _const_spec(wq.shape),
            _const_spec((1, KV_LORA_RANK)), _const_spec(wkv.shape),
            tab_spec(Q_PE_WIDTH), tab_spec(Q_PE_WIDTH), tab_spec(HEAD_PAD), tab_spec(HEAD_PAD),
            _const_spec((CONV_WIDTH, RNN_WIDTH)), _const_spec((1, RNN_WIDTH)),
            _const_spec(wgate.shape), _const_spec((2, RNN_WIDTH)),
            _const_spec((1, RNN_WIDTH)), _const_spec((1, RNN_WIDTH)),
        ],
        out_specs=[
            pl.BlockSpec((1, ts // tq, MLA_HEADS, HEAD_PAD, tq), lambda bi, si: (bi, si, 0, 0, 0)),
            pl.BlockSpec((1, MLA_HEADS, ts, HEAD_PAD), lambda bi, si: (bi, 0, si, 0)),
            pl.BlockSpec((1, ts // tq, MLA_HEADS, V_EXT, tq), lambda bi, si: (bi, si, 0, 0, 0)),
            seq_spec(MLA_WIDTH), seq_spec(RNN_WIDTH),
        ],
        out_shape=[
            jax.ShapeDtypeStruct((b, s // tq, MLA_HEADS, HEAD_PAD, tq), bf16),
            jax.ShapeDtypeStruct((b, MLA_HEADS, s, HEAD_PAD), bf16),
            jax.ShapeDtypeStruct((b, s // tq, MLA_HEADS, V_EXT, tq), bf16),
            jax.ShapeDtypeStruct((b, s, MLA_WIDTH), bf16),
            jax.ShapeDtypeStruct((b, s, RNN_WIDTH), bf16),
        ],
        scratch_shapes=[pltpu.VMEM((CONV_WIDTH - 1, SUBLANES, RNN_WIDTH), f32),
                        pltpu.VMEM((1, RNN_WIDTH), f32),
                        pltpu.VMEM((RNN_WIDTH // LANES, SUBLANES * PERM_PITCH, LANES), f32)],
        compiler_params=params,
        name="proj_rnn",
    )(x, row(norm_in_g[0]), w_in_r, row(q_norm_g[0]), wq, row(kv_norm_g[0]), wkv,
      cq, sq, ck, sk, conv_w[0].astype(f32), row(conv_b[0]), wgate, bgate,
      row(lru_param[0]), row(out_norm_rnn_g[0]))

    yt = pl.pallas_call(
        _attn_kernel,
        grid=(b, s // tq),
        in_specs=[
            pl.BlockSpec((1, 1, MLA_HEADS, HEAD_PAD, tq), lambda bi, qi: (bi, qi, 0, 0, 0)),
            pl.BlockSpec((1, MLA_HEADS, s, HEAD_PAD), lambda bi, qi: (bi, 0, 0, 0)),
            pl.BlockSpec((1, s // tq, MLA_HEADS, V_EXT, tq), lambda bi, qi: (bi, 0, 0, 0, 0)),
        ],
        out_specs=pl.BlockSpec((1, MLA_WIDTH, tq), lambda bi, qi: (bi, 0, qi)),
        out_shape=jax.ShapeDtypeStruct((b, MLA_WIDTH, s), f32),
        scratch_shapes=[pltpu.VMEM((MLA_HEADS, tq, tq), f32),
                        pltpu.VMEM((MLA_HEADS, tq, tq), f32),
                        pltpu.VMEM((MLA_HEADS, 1, tq), f32),
                        pltpu.VMEM((MLA_HEADS, 1, tq), f32),
                        pltpu.VMEM((MLA_HEADS, 1, tq), f32),
                        pltpu.VMEM((MLA_HEADS, V_EXT, tq), f32)],
        compiler_params=params,
        name="mla_attention",
    )(qt, k, vt)

    return pl.pallas_call(
        _out_kernel,
        grid=(b, s // ts),
        in_specs=[
            pl.BlockSpec((1, MLA_WIDTH, ts), lambda bi, si: (bi, 0, si)),
            seq_spec(MLA_WIDTH), seq_spec(RNN_WIDTH), seq_spec(D_MODEL),
            _const_spec((1, MLA_WIDTH)), _const_spec((D_MODEL, D_MODEL)),
            _const_spec((1, D_MODEL)),
        ],
        out_specs=seq_spec(D_MODEL),
        out_shape=jax.ShapeDtypeStruct((b, s, D_MODEL), f32),
        compiler_params=params,
        name="out_proj",
    )(yt, gm, yr, x, row(out_norm_mla_g[0]), w_out[0].astype(bf16), row(final_norm_g))
```

```python
import functools
import math

import jax
import jax.numpy as jnp
import numpy as np
from jax import lax
from jax.experimental import pallas as pl
from jax.experimental.pallas import tpu as pltpu

D_MODEL = 1024
CHUNK = 64
MLA_HEADS = 8
QK_NOPE_DIM = 64
QK_ROPE_DIM = 32
QK_HEAD_DIM = QK_NOPE_DIM + QK_ROPE_DIM
V_HEAD_DIM = 64
MLA_WIDTH = MLA_HEADS * V_HEAD_DIM
Q_LORA_RANK = 256
KV_LORA_RANK = 128
ROPE_THETA = 10000.0
RNN_WIDTH = 512
RNN_BLOCKS = 8
RNN_BLOCK_DIM = RNN_WIDTH // RNN_BLOCKS
CONV_WIDTH = 4
LRU_C = 8.0
NORM_EPS = 1e-6

LANES = 128
SUBLANES = 8
MXU_DIM = 256
HEAD_PAD = LANES
V_EXT = V_HEAD_DIM + 16
Z_WIDTH = 2048
SEQ_TILE = 512
OUT_TILE = 1024
ATT_BLOCK = MXU_DIM
VMEM_LIMIT = 56 * 1024 * 1024

Z_Q, Z_KV, Z_KPE, Z_GM, Z_XR, Z_GR = 0, 256, 384, 512, 1024, 1536
Q_NOPE, Q_PE, Q_ROT = 0, MLA_HEADS * QK_NOPE_DIM, MLA_HEADS * (QK_NOPE_DIM + QK_ROPE_DIM)
Q_PE_WIDTH = MLA_HEADS * QK_ROPE_DIM
GATE_GROUP = MXU_DIM
PERM_PITCH = SEQ_TILE // SUBLANES + SUBLANES

MASK_VALUE = -0.5 * float(jnp.finfo(jnp.float32).max)

f32 = jnp.float32
bf16 = jnp.bfloat16


def _rms(x, g):
    ms = jnp.mean(x * x, axis=-1, keepdims=True)
    return x * lax.rsqrt(ms + NORM_EPS) * g


def _sigmoid(x):
    return 0.5 * jnp.tanh(0.5 * x) + 0.5


def _silu(x):
    hx = 0.5 * x
    return hx * jnp.tanh(hx) + hx


def _to_lockstep(x, buf):
    rows = x.shape[0]
    seg = rows // SUBLANES
    slabs = x.shape[1] // LANES
    for c in range(slabs):
        for r in range(SUBLANES):
            buf[c, r * PERM_PITCH:r * PERM_PITCH + seg, :] = (
                x[r * seg:(r + 1) * seg, c * LANES:(c + 1) * LANES])
    return [jnp.concatenate([buf[c, pl.ds(i, SUBLANES, stride=PERM_PITCH), :]
                             for c in range(slabs)], axis=1) for i in range(seg)]


def _from_lockstep(pieces, buf):
    seg = len(pieces)
    slabs = pieces[0].shape[1] // LANES
    for i, piece in enumerate(pieces):
        for c in range(slabs):
            buf[c, pl.ds(i, SUBLANES, stride=PERM_PITCH), :] = piece[:, c * LANES:(c + 1) * LANES]
    return jnp.concatenate(
        [jnp.concatenate([buf[c, r * PERM_PITCH:r * PERM_PITCH + seg, :] for c in range(slabs)],
                         axis=1) for r in range(SUBLANES)], axis=0)


def _proj_kernel(x_ref, gin_ref, win_ref, qg_ref, wq_ref, kvg_ref, wkv_ref,
                 cq_ref, sq_ref, ck_ref, sk_ref, convw_ref, convb_ref, wgate_ref,
                 bgate_ref, lru_ref, grnn_ref,
                 qt_out, k_out, vt_out, gm_out, yr_out,
                 xr_tail, h_carry, perm_buf):
    s = pl.program_id(1)
    ts = x_ref.shape[1]

    @pl.when(s == 0)
    def _():
        xr_tail[...] = jnp.zeros(xr_tail.shape, f32)
        h_carry[...] = jnp.zeros(h_carry.shape, f32)

    h = _rms(x_ref[0], gin_ref[...]).astype(bf16)

    def in_proj(lo, width):
        return jnp.dot(h, win_ref[:, lo:lo + width], preferred_element_type=f32)

    z_xr = in_proj(Z_XR, RNN_WIDTH)
    z_a = in_proj(0, Z_GM)
    xb, xr = _conv_lockstep(z_xr, xr_tail, perm_buf, convw_ref, convb_ref)
    gates = [jnp.dot(xb[:, g * GATE_GROUP:(g + 1) * GATE_GROUP], wgate_ref[g],
                     preferred_element_type=f32)
             for g in range(RNN_WIDTH // GATE_GROUP)]
    z_gm = in_proj(Z_GM, MLA_WIDTH)
    qn = _rms(z_a[:, Z_Q:Z_Q + Q_LORA_RANK], qg_ref[...]).astype(bf16)
    q = jnp.dot(qn, wq_ref[...], preferred_element_type=f32)
    kvn = _rms(z_a[:, Z_KV:Z_KV + KV_LORA_RANK], kvg_ref[...]).astype(bf16)
    kv = jnp.dot(kvn, wkv_ref[...], preferred_element_type=f32)
    z_gr = in_proj(Z_GR, RNN_WIDTH)

    y = _rg_lru(s, gates, xr, bgate_ref, lru_ref, h_carry, perm_buf)
    yr_out[0] = (_rms(y, grnn_ref[...]) * _silu(z_gr)).astype(bf16)

    scale = math.log2(math.e) / math.sqrt(QK_HEAD_DIM)
    q_nope_t = (q[:, Q_NOPE:Q_PE] * scale).T
    q_pe = q[:, Q_PE:Q_ROT] * cq_ref[...] + q[:, Q_ROT:] * sq_ref[...]
    q_pe_t = q_pe.T
    pad = jnp.zeros((HEAD_PAD - QK_HEAD_DIM, ATT_BLOCK), bf16)
    for j in range(ts // ATT_BLOCK):
        cols = slice(j * ATT_BLOCK, (j + 1) * ATT_BLOCK)
        for hh in range(MLA_HEADS):
            qt_out[0, j, hh, :QK_NOPE_DIM, :] = (
                q_nope_t[hh * QK_NOPE_DIM:(hh + 1) * QK_NOPE_DIM, cols].astype(bf16))
            qt_out[0, j, hh, QK_NOPE_DIM:QK_HEAD_DIM, :] = (
                q_pe_t[hh * QK_ROPE_DIM:(hh + 1) * QK_ROPE_DIM, cols].astype(bf16))
            qt_out[0, j, hh, QK_HEAD_DIM:, :] = pad

    kblk = z_a[:, Z_KPE:Z_KPE + HEAD_PAD]
    kpe = kblk * ck_ref[...] + pltpu.roll(kblk, LANES - QK_ROPE_DIM, 1) * sk_ref[...]
    for hh in range(MLA_HEADS):
        k_out[0, hh] = (kv[:, hh * HEAD_PAD:(hh + 1) * HEAD_PAD] + kpe).astype(bf16)
    vt = kv[:, MLA_HEADS * HEAD_PAD:].T
    ones_row = lax.broadcasted_iota(jnp.int32, (MLA_HEADS, V_EXT - V_HEAD_DIM, ATT_BLOCK), 1) == 0
    for j in range(ts // ATT_BLOCK):
        blk = vt[:, j * ATT_BLOCK:(j + 1) * ATT_BLOCK]
        vt_out[0, j, :, :V_HEAD_DIM, :] = (
            blk.reshape(MLA_HEADS, V_HEAD_DIM, ATT_BLOCK).astype(bf16))
        vt_out[0, j, :, V_HEAD_DIM:, :] = jnp.where(ones_row, 1.0, 0.0).astype(bf16)

    gm_out[0] = _silu(z_gm).astype(bf16)


def _conv_lockstep(x_rnn, xr_tail, perm_buf, convw_ref, convb_ref):
    ts = x_rnn.shape[0]
    sub8 = lax.broadcasted_iota(jnp.int32, (SUBLANES, RNN_WIDTH), 0)
    xl = _to_lockstep(x_rnn, perm_buf)
    hist = []
    for k in range(CONV_WIDTH - 1):
        prev = xr_tail[k]
        cur = xl[len(xl) - (CONV_WIDTH - 1) + k]
        hist.append(pltpu.roll(jnp.where(sub8 == SUBLANES - 1, prev, cur), 1, 0))
        xr_tail[k] = cur
    xext = jnp.concatenate(hist + xl, axis=0)
    xr = convb_ref[...]
    for k in range(CONV_WIDTH):
        xr = xr + convw_ref[k:k + 1, :] * xext[k * SUBLANES:k * SUBLANES + ts]
    return xr.astype(bf16), xr


def _rg_lru(s, gates, xr, bgate_ref, lru_ref, h_carry, perm_buf):
    ts = xr.shape[0]
    sub8 = lax.broadcasted_iota(jnp.int32, (SUBLANES, RNN_WIDTH), 0)
    ga = [gg[:, :GATE_GROUP] for gg in gates]
    gx = [gg[:, GATE_GROUP:] for gg in gates]
    r = _sigmoid(jnp.concatenate(ga, axis=1) + bgate_ref[0:1, :])
    i = _sigmoid(jnp.concatenate(gx, axis=1) + bgate_ref[1:2, :])
    nl = -lru_ref[...]
    softplus = jnp.maximum(nl, 0.0) + jnp.log1p(jnp.exp(-jnp.abs(nl)))
    a = jnp.exp2(r * ((-LRU_C * math.log2(math.e)) * softplus))
    m2 = jnp.maximum(1.0 - a * a, 1e-12)
    mult = m2 * lax.rsqrt(m2)
    row = lax.broadcasted_iota(jnp.int32, (ts, RNN_WIDTH), 0)
    mult = jnp.where((row == 0) & (s == 0), 1.0, mult)
    u = mult * (i * xr)

    steps = ts // SUBLANES
    hl, pr = [], []
    hcur = pcur = None
    for t in range(steps):
        at = a[t * SUBLANES:(t + 1) * SUBLANES]
        ut = u[t * SUBLANES:(t + 1) * SUBLANES]
        hcur = ut if t == 0 else at * hcur + ut
        pcur = at if t == 0 else at * pcur
        hl.append(hcur)
        pr.append(pcur)
    pe, he = pcur, hcur
    for d in (1, 2, 4):
        keep = sub8 >= d
        he = jnp.where(keep, pe * pltpu.roll(he, d, 0) + he, he)
        pe = jnp.where(keep, pe * pltpu.roll(pe, d, 0), pe)
    carry = h_carry[...]
    ends = he + pe * carry
    h_carry[...] = ends[SUBLANES - 1:SUBLANES, :]
    init = jnp.where(sub8 == 0, carry, pltpu.roll(ends, 1, 0))
    return _from_lockstep([hl[t] + pr[t] * init for t in range(steps)], perm_buf)


def _attn_kernel(qt_ref, k_ref, vt_ref, gm_ref, gmla_ref, o_ref,
                 s_even, s_odd, c_even, c_odd, m_ref, acc_ref):
    qi = pl.program_id(1)
    tq = ATT_BLOCK
    even = (s_even, c_even)
    odd = (s_odd, c_odd)

    m_ref[...] = jnp.full(m_ref.shape, MASK_VALUE, f32)
    acc_ref[...] = jnp.zeros(acc_ref.shape, f32)

    def qk(hh, j, dst, diagonal):
        s_dst, c_dst = dst
        kb = k_ref[0, hh, pl.ds(pl.multiple_of(j * tq, tq), tq), :]
        st = jnp.dot(kb, qt_ref[0, 0, hh], preferred_element_type=f32)
        if diagonal:
            kc = lax.broadcasted_iota(jnp.int32, st.shape, 0) // CHUNK
            qc = lax.broadcasted_iota(jnp.int32, st.shape, 1) // CHUNK
            st = jnp.where(kc <= qc, st, MASK_VALUE)
        s_dst[hh] = st
        c_dst[hh] = jnp.max(st, axis=0, keepdims=True)

    def attend(hh, j, src):
        s_src, c_src = src
        m_prev = m_ref[hh]
        m_new = jnp.maximum(m_prev, c_src[hh])
        alpha = jnp.exp2(m_prev - m_new)
        p = jnp.exp2(s_src[hh] - m_new).astype(bf16)
        upd = jnp.dot(vt_ref[0, j, hh], p, preferred_element_type=f32)
        acc_ref[hh] = alpha * acc_ref[hh] + upd
        m_ref[hh] = m_new

    def step(j, cur, nxt, next_is_diagonal):
        for hh in range(MLA_HEADS):
            qk(hh, j + 1, nxt, next_is_diagonal)
            attend(hh, j, cur)

    def finish(cur):
        heads = []
        for hh in range(MLA_HEADS):
            attend(hh, qi, cur)
            acc = acc_ref[hh]
            heads.append(acc[:V_HEAD_DIM] / acc[V_HEAD_DIM:V_HEAD_DIM + 1])
        y = jnp.concatenate(heads, axis=0).T
        o_ref[0] = (_rms(y, gmla_ref[...]) * gm_ref[0].astype(f32)).astype(bf16)

    @pl.when(qi == 0)
    def _():
        for hh in range(MLA_HEADS):
            qk(hh, 0, even, True)
        finish(even)

    @pl.when(qi > 0)
    def _():
        for hh in range(MLA_HEADS):
            qk(hh, 0, even, False)

    def body(j, carry):
        @pl.when(lax.rem(j, 2) == 0)
        def _():
            step(j, even, odd, False)

        @pl.when(lax.rem(j, 2) == 1)
        def _():
            step(j, odd, even, False)
        return carry

    lax.fori_loop(0, qi - 1, body, 0)

    @pl.when((qi > 0) & (lax.rem(qi, 2) == 1))
    def _():
        step(qi - 1, even, odd, True)
        finish(odd)

    @pl.when((qi > 0) & (lax.rem(qi, 2) == 0))
    def _():
        step(qi - 1, odd, even, True)
        finish(even)


def _out_kernel(ym_ref, yr_ref, x_ref, wout_ref, gfin_ref, o_ref):
    y = jnp.concatenate([ym_ref[0], yr_ref[0]], axis=-1)
    xo = x_ref[0] + jnp.dot(y, wout_ref[...], preferred_element_type=f32)
    o_ref[0] = _rms(xo, gfin_ref[...])


def _rot_half(w):
    half = QK_ROPE_DIM // 2
    return jnp.concatenate([-w[..., half:], w[..., :half]], axis=-1)


def _prepare_weights(w_in, w_q_b, w_kv_b, w_rg_a, w_rg_x):
    offs = [0, 256, 384, 416, 928, 1440, 1952]
    w_qlat, w_kvlat, w_krope, w_gm, w_xr, w_gr = (
        w_in[:, offs[i]:offs[i + 1]] for i in range(6))
    kpe_blk = jnp.concatenate(
        [jnp.zeros((D_MODEL, QK_NOPE_DIM), w_in.dtype), w_krope, _rot_half(w_krope)], axis=1)
    w_in_r = jnp.concatenate([w_qlat, w_kvlat, kpe_blk, w_gm, w_xr, w_gr], axis=1)

    wq = w_q_b.reshape(Q_LORA_RANK, MLA_HEADS, QK_HEAD_DIM)
    wq_nope = wq[:, :, :QK_NOPE_DIM].reshape(Q_LORA_RANK, -1)
    wq_pe = wq[:, :, QK_NOPE_DIM:]
    wq_r = jnp.concatenate([wq_nope, wq_pe.reshape(Q_LORA_RANK, -1),
                            _rot_half(wq_pe).reshape(Q_LORA_RANK, -1)], axis=1)

    wkv = w_kv_b.reshape(KV_LORA_RANK, MLA_HEADS, QK_NOPE_DIM + V_HEAD_DIM)
    wk = jnp.pad(wkv[:, :, :QK_NOPE_DIM], ((0, 0), (0, 0), (0, HEAD_PAD - QK_NOPE_DIM)))
    wk = wk.reshape(KV_LORA_RANK, MLA_HEADS * HEAD_PAD)
    wv = wkv[:, :, QK_NOPE_DIM:].reshape(KV_LORA_RANK, MLA_WIDTH)
    wkv_r = jnp.concatenate([wk, wv], axis=1)

    per = GATE_GROUP // RNN_BLOCK_DIM
    eye = jnp.eye(per, dtype=w_rg_a.dtype)

    def block_diag(w):
        w = w.reshape(RNN_BLOCKS // per, per, RNN_BLOCK_DIM, RNN_BLOCK_DIM)
        return jnp.einsum('gncd,nm->gncmd', w, eye).reshape(-1, GATE_GROUP, GATE_GROUP)

    wgate = jnp.concatenate([block_diag(w_rg_a), block_diag(w_rg_x)], axis=2)
    return (w_in_r.astype(bf16), wq_r.astype(bf16), wkv_r.astype(bf16), wgate.astype(bf16))


@functools.lru_cache(maxsize=None)
def _rope_tables(seq):
    pos = np.arange(seq, dtype=np.float64)
    inv_freq = ROPE_THETA ** (-np.arange(0, QK_ROPE_DIM, 2, dtype=np.float64) / QK_ROPE_DIM)
    ang = pos[:, None] * inv_freq[None, :]
    cos = np.tile(np.cos(ang), (1, 2))
    sin = np.tile(np.sin(ang), (1, 2))
    scale = math.log2(math.e) / math.sqrt(QK_HEAD_DIM)
    cq = np.tile(cos, (1, MLA_HEADS)) * scale
    sq = np.tile(sin, (1, MLA_HEADS)) * scale
    pads = ((0, 0), (QK_NOPE_DIM, HEAD_PAD - QK_HEAD_DIM))
    return tuple(t.astype(np.float32) for t in (cq, sq, np.pad(cos, pads), np.pad(sin, pads)))


def _const_spec(shape):
    return pl.BlockSpec(shape, lambda *_: (0,) * len(shape))


def kernel(x, norm_in_g, w_in, q_norm_g, w_q_b, kv_norm_g, w_kv_b, conv_w, conv_b,
           w_rg_a, b_rg_a, w_rg_x, b_rg_x, lru_param, out_norm_mla_g, out_norm_rnn_g,
           w_out, final_norm_g):
    b, s, _ = x.shape
    ts = SEQ_TILE
    tq = ATT_BLOCK
    assert s % ts == 0 and ts % tq == 0 and tq % CHUNK == 0 and s % OUT_TILE == 0
    assert w_in.shape[0] == 1, "the final norm is fused into the single layer's output kernel"
    cq, sq, ck, sk = _rope_tables(s)
    row = lambda v: v.reshape(1, -1).astype(f32)
    params = pltpu.CompilerParams(
        dimension_semantics=("arbitrary", "arbitrary"), vmem_limit_bytes=VMEM_LIMIT)

    w_in_r, wq, wkv, wgate = _prepare_weights(w_in[0], w_q_b[0], w_kv_b[0], w_rg_a[0], w_rg_x[0])
    bgate = jnp.stack([b_rg_a[0], b_rg_x[0]]).astype(f32)

    seq_spec = lambda w: pl.BlockSpec((1, ts, w), lambda bi, si: (bi, si, 0))
    tab_spec = lambda w: pl.BlockSpec((ts, w), lambda bi, si: (si, 0))
    qt, k, vt, gm, yr = pl.pallas_call(
        _proj_kernel,
        grid=(b, s // ts),
        in_specs=[
            seq_spec(D_MODEL), _const_spec((1, D_MODEL)), _const_spec((D_MODEL, Z_WIDTH)),
            _const_spec((1, Q_LORA_RANK)), _const_spec(wq.shape),
            _const_spec((1, KV_LORA_RANK)), _const_spec(wkv.shape),
            tab_spec(Q_PE_WIDTH), tab_spec(Q_PE_WIDTH), tab_spec(HEAD_PAD), tab_spec(HEAD_PAD),
            _const_spec((CONV_WIDTH, RNN_WIDTH)), _const_spec((1, RNN_WIDTH)),
            _const_spec(wgate.shape), _const_spec((2, RNN_WIDTH)),
            _const_spec((1, RNN_WIDTH)), _const_spec((1, RNN_WIDTH)),
        ],
        out_specs=[
            pl.BlockSpec((1, ts // tq, MLA_HEADS, HEAD_PAD, tq), lambda bi, si: (bi, si, 0, 0, 0)),
            pl.BlockSpec((1, MLA_HEADS, ts, HEAD_PAD), lambda bi, si: (bi, 0, si, 0)),
            pl.BlockSpec((1, ts // tq, MLA_HEADS, V_EXT, tq), lambda bi, si: (bi, si, 0, 0, 0)),
            seq_spec(MLA_WIDTH), seq_spec(RNN_WIDTH),
        ],
        out_shape=[
            jax.ShapeDtypeStruct((b, s // tq, MLA_HEADS, HEAD_PAD, tq), bf16),
            jax.ShapeDtypeStruct((b, MLA_HEADS, s, HEAD_PAD), bf16),
            jax.ShapeDtypeStruct((b, s // tq, MLA_HEADS, V_EXT, tq), bf16),
            jax.ShapeDtypeStruct((b, s, MLA_WIDTH), bf16),
            jax.ShapeDtypeStruct((b, s, RNN_WIDTH), bf16),
        ],
        scratch_shapes=[pltpu.VMEM((CONV_WIDTH - 1, SUBLANES, RNN_WIDTH), f32),
                        pltpu.VMEM((1, RNN_WIDTH), f32),
                        pltpu.VMEM((RNN_WIDTH // LANES, SUBLANES * PERM_PITCH, LANES), f32)],
        compiler_params=params,
        name="proj_rnn",
    )(x, row(norm_in_g[0]), w_in_r, row(q_norm_g[0]), wq, row(kv_norm_g[0]), wkv,
      cq, sq, ck, sk, conv_w[0].astype(f32), row(conv_b[0]), wgate, bgate,
      row(lru_param[0]), row(out_norm_rnn_g[0]))

    ym = pl.pallas_call(
        _attn_kernel,
        grid=(b, s // tq),
        in_specs=[
            pl.BlockSpec((1, 1, MLA_HEADS, HEAD_PAD, tq), lambda bi, qi: (bi, qi, 0, 0, 0)),
            pl.BlockSpec((1, MLA_HEADS, s, HEAD_PAD), lambda bi, qi: (bi, 0, 0, 0)),
            pl.BlockSpec((1, s // tq, MLA_HEADS, V_EXT, tq), lambda bi, qi: (bi, 0, 0, 0, 0)),
            pl.BlockSpec((1, tq, MLA_WIDTH), lambda bi, qi: (bi, qi, 0)),
            _const_spec((1, MLA_WIDTH)),
        ],
        out_specs=pl.BlockSpec((1, tq, MLA_WIDTH), lambda bi, qi: (bi, qi, 0)),
        out_shape=jax.ShapeDtypeStruct((b, s, MLA_WIDTH), bf16),
        scratch_shapes=[pltpu.VMEM((MLA_HEADS, tq, tq), f32),
                        pltpu.VMEM((MLA_HEADS, tq, tq), f32),
                        pltpu.VMEM((MLA_HEADS, 1, tq), f32),
                        pltpu.VMEM((MLA_HEADS, 1, tq), f32),
                        pltpu.VMEM((MLA_HEADS, 1, tq), f32),
                        pltpu.VMEM((MLA_HEADS, V_EXT, tq), f32)],
        compiler_params=params,
        name="mla_attention",
    )(qt, k, vt, gm, row(out_norm_mla_g[0]))

    to = OUT_TILE
    out_spec = lambda w: pl.BlockSpec((1, to, w), lambda bi, si: (bi, si, 0))
    return pl.pallas_call(
        _out_kernel,
        grid=(b, s // to),
        in_specs=[
            out_spec(MLA_WIDTH), out_spec(RNN_WIDTH), out_spec(D_MODEL),
            _const_spec((D_MODEL, D_MODEL)), _const_spec((1, D_MODEL)),
        ],
        out_specs=out_spec(D_MODEL),
        out_shape=jax.ShapeDtypeStruct((b, s, D_MODEL), f32),
        compiler_params=params,
        name="out_proj",
    )(ym, yr, x, w_out[0].astype(bf16), row(final_norm_g))
```

```python
import functools
import math

import jax
import jax.numpy as jnp
import numpy as np
from jax import lax
from jax.experimental import pallas as pl
from jax.experimental.pallas import tpu as pltpu

D_MODEL = 1024
CHUNK = 64
MLA_HEADS = 8
QK_NOPE_DIM = 64
QK_ROPE_DIM = 32
QK_HEAD_DIM = QK_NOPE_DIM + QK_ROPE_DIM
V_HEAD_DIM = 64
MLA_WIDTH = MLA_HEADS * V_HEAD_DIM
Q_LORA_RANK = 256
KV_LORA_RANK = 128
ROPE_THETA = 10000.0
RNN_WIDTH = 512
RNN_BLOCKS = 8
RNN_BLOCK_DIM = RNN_WIDTH // RNN_BLOCKS
CONV_WIDTH = 4
LRU_C = 8.0
NORM_EPS = 1e-6

LANES = 128
SUBLANES = 8
MXU_DIM = 256
HEAD_PAD = LANES
V_EXT = V_HEAD_DIM + 16
Z_WIDTH = 2048
SEQ_TILE = 512
OUT_TILE = 1024
ATT_BLOCK = MXU_DIM
VMEM_LIMIT = 56 * 1024 * 1024

Z_Q, Z_KV, Z_KPE, Z_GM, Z_XR, Z_GR = 0, 256, 384, 512, 1024, 1536
Q_NOPE, Q_PE, Q_ROT = 0, MLA_HEADS * QK_NOPE_DIM, MLA_HEADS * (QK_NOPE_DIM + QK_ROPE_DIM)
Q_PE_WIDTH = MLA_HEADS * QK_ROPE_DIM
GATE_GROUP = MXU_DIM
PERM_PITCH = SEQ_TILE // SUBLANES + SUBLANES

MASK_VALUE = -0.5 * float(jnp.finfo(jnp.float32).max)

f32 = jnp.float32
bf16 = jnp.bfloat16


def _rms(x, g):
    ms = jnp.mean(x * x, axis=-1, keepdims=True)
    return x * lax.rsqrt(ms + NORM_EPS) * g


def _sigmoid(x):
    return 0.5 * jnp.tanh(0.5 * x) + 0.5


def _silu(x):
    hx = 0.5 * x
    return hx * jnp.tanh(hx) + hx


def _to_lockstep(x, buf):
    rows = x.shape[0]
    seg = rows // SUBLANES
    slabs = x.shape[1] // LANES
    for c in range(slabs):
        for r in range(SUBLANES):
            buf[c, r * PERM_PITCH:r * PERM_PITCH + seg, :] = (
                x[r * seg:(r + 1) * seg, c * LANES:(c + 1) * LANES])
    return [jnp.concatenate([buf[c, pl.ds(i, SUBLANES, stride=PERM_PITCH), :]
                             for c in range(slabs)], axis=1) for i in range(seg)]


def _from_lockstep(pieces, buf):
    seg = len(pieces)
    slabs = pieces[0].shape[1] // LANES
    for i, piece in enumerate(pieces):
        for c in range(slabs):
            buf[c, pl.ds(i, SUBLANES, stride=PERM_PITCH), :] = piece[:, c * LANES:(c + 1) * LANES]
    return jnp.concatenate(
        [jnp.concatenate([buf[c, r * PERM_PITCH:r * PERM_PITCH + seg, :] for c in range(slabs)],
                         axis=1) for r in range(SUBLANES)], axis=0)


def _proj_kernel(x_ref, gin_ref, win_ref, qg_ref, wq_ref, kvg_ref, wkv_ref,
                 cq_ref, sq_ref, ck_ref, sk_ref, convw_ref, convb_ref, wgate_ref,
                 bgate_ref, lru_ref, grnn_ref,
                 qt_out, k_out, vt_out, gm_out, yr_out,
                 xr_tail, h_carry, perm_buf):
    s = pl.program_id(1)
    ts = x_ref.shape[1]

    @pl.when(s == 0)
    def _():
        xr_tail[...] = jnp.zeros(xr_tail.shape, f32)
        h_carry[...] = jnp.zeros(h_carry.shape, f32)

    h = _rms(x_ref[0], gin_ref[...]).astype(bf16)

    def in_proj(lo, width):
        return jnp.dot(h, win_ref[:, lo:lo + width], preferred_element_type=f32)

    z_xr = in_proj(Z_XR, RNN_WIDTH)
    z_a = in_proj(0, Z_GM)
    xb, xr = _conv_lockstep(z_xr, xr_tail, perm_buf, convw_ref, convb_ref)
    gates = [jnp.dot(xb[:, g * GATE_GROUP:(g + 1) * GATE_GROUP], wgate_ref[g],
                     preferred_element_type=f32)
             for g in range(RNN_WIDTH // GATE_GROUP)]
    z_gm = in_proj(Z_GM, MLA_WIDTH)
    qn = _rms(z_a[:, Z_Q:Z_Q + Q_LORA_RANK], qg_ref[...]).astype(bf16)
    q = jnp.dot(qn, wq_ref[...], preferred_element_type=f32)
    kvn = _rms(z_a[:, Z_KV:Z_KV + KV_LORA_RANK], kvg_ref[...]).astype(bf16)
    kv = jnp.dot(kvn, wkv_ref[...], preferred_element_type=f32)
    z_gr = in_proj(Z_GR, RNN_WIDTH)

    y = _rg_lru(s, gates, xr, bgate_ref, lru_ref, h_carry, perm_buf)
    yr_out[0] = (_rms(y, grnn_ref[...]) * _silu(z_gr)).astype(bf16)

    scale = math.log2(math.e) / math.sqrt(QK_HEAD_DIM)
    q_nope_t = (q[:, Q_NOPE:Q_PE] * scale).T
    q_pe = q[:, Q_PE:Q_ROT] * cq_ref[...] + q[:, Q_ROT:] * sq_ref[...]
    q_pe_t = q_pe.T
    pad = jnp.zeros((HEAD_PAD - QK_HEAD_DIM, ATT_BLOCK), bf16)
    for j in range(ts // ATT_BLOCK):
        cols = slice(j * ATT_BLOCK, (j + 1) * ATT_BLOCK)
        for hh in range(MLA_HEADS):
            qt_out[0, j, hh, :QK_NOPE_DIM, :] = (
                q_nope_t[hh * QK_NOPE_DIM:(hh + 1) * QK_NOPE_DIM, cols].astype(bf16))
            qt_out[0, j, hh, QK_NOPE_DIM:QK_HEAD_DIM, :] = (
                q_pe_t[hh * QK_ROPE_DIM:(hh + 1) * QK_ROPE_DIM, cols].astype(bf16))
            qt_out[0, j, hh, QK_HEAD_DIM:, :] = pad

    kblk = z_a[:, Z_KPE:Z_KPE + HEAD_PAD]
    kpe = kblk * ck_ref[...] + pltpu.roll(kblk, LANES - QK_ROPE_DIM, 1) * sk_ref[...]
    for hh in range(MLA_HEADS):
        k_out[0, hh] = (kv[:, hh * HEAD_PAD:(hh + 1) * HEAD_PAD] + kpe).astype(bf16)
    vt = kv[:, MLA_HEADS * HEAD_PAD:].T
    ones_row = lax.broadcasted_iota(jnp.int32, (MLA_HEADS, V_EXT - V_HEAD_DIM, ATT_BLOCK), 1) == 0
    for j in range(ts // ATT_BLOCK):
        blk = vt[:, j * ATT_BLOCK:(j + 1) * ATT_BLOCK]
        vt_out[0, j, :, :V_HEAD_DIM, :] = (
            blk.reshape(MLA_HEADS, V_HEAD_DIM, ATT_BLOCK).astype(bf16))
        vt_out[0, j, :, V_HEAD_DIM:, :] = jnp.where(ones_row, 1.0, 0.0).astype(bf16)

    gm_out[0] = _silu(z_gm).astype(bf16)


def _conv_lockstep(x_rnn, xr_tail, perm_buf, convw_ref, convb_ref):
    ts = x_rnn.shape[0]
    sub8 = lax.broadcasted_iota(jnp.int32, (SUBLANES, RNN_WIDTH), 0)
    xl = _to_lockstep(x_rnn, perm_buf)
    hist = []
    for k in range(CONV_WIDTH - 1):
        prev = xr_tail[k]
        cur = xl[len(xl) - (CONV_WIDTH - 1) + k]
        hist.append(pltpu.roll(jnp.where(sub8 == SUBLANES - 1, prev, cur), 1, 0))
        xr_tail[k] = cur
    xext = jnp.concatenate(hist + xl, axis=0)
    xr = convb_ref[...]
    for k in range(CONV_WIDTH):
        xr = xr + convw_ref[k:k + 1, :] * xext[k * SUBLANES:k * SUBLANES + ts]
    return xr.astype(bf16), xr


def _rg_lru(s, gates, xr, bgate_ref, lru_ref, h_carry, perm_buf):
    ts = xr.shape[0]
    sub8 = lax.broadcasted_iota(jnp.int32, (SUBLANES, RNN_WIDTH), 0)
    ga = [gg[:, :GATE_GROUP] for gg in gates]
    gx = [gg[:, GATE_GROUP:] for gg in gates]
    r = _sigmoid(jnp.concatenate(ga, axis=1) + bgate_ref[0:1, :])
    i = _sigmoid(jnp.concatenate(gx, axis=1) + bgate_ref[1:2, :])
    nl = -lru_ref[...]
    softplus = jnp.maximum(nl, 0.0) + jnp.log1p(jnp.exp(-jnp.abs(nl)))
    a = jnp.exp2(r * ((-LRU_C * math.log2(math.e)) * softplus))
    m2 = jnp.maximum(1.0 - a * a, 1e-12)
    mult = m2 * lax.rsqrt(m2)
    row = lax.broadcasted_iota(jnp.int32, (ts, RNN_WIDTH), 0)
    mult = jnp.where((row == 0) & (s == 0), 1.0, mult)
    u = mult * (i * xr)

    steps = ts // SUBLANES
    hl, pr = [], []
    hcur = pcur = None
    for t in range(steps):
        at = a[t * SUBLANES:(t + 1) * SUBLANES]
        ut = u[t * SUBLANES:(t + 1) * SUBLANES]
        hcur = ut if t == 0 else at * hcur + ut
        pcur = at if t == 0 else at * pcur
        hl.append(hcur)
        pr.append(pcur)
    pe, he = pcur, hcur
    for d in (1, 2, 4):
        keep = sub8 >= d
        he = jnp.where(keep, pe * pltpu.roll(he, d, 0) + he, he)
        pe = jnp.where(keep, pe * pltpu.roll(pe, d, 0), pe)
    carry = h_carry[...]
    ends = he + pe * carry
    h_carry[...] = ends[SUBLANES - 1:SUBLANES, :]
    init = jnp.where(sub8 == 0, carry, pltpu.roll(ends, 1, 0))
    return _from_lockstep([hl[t] + pr[t] * init for t in range(steps)], perm_buf)


def _attn_kernel(qt_ref, k_ref, vt_ref, gm_ref, gmla_ref, o_ref,
                 s_even, s_odd, c_even, c_odd, m_ref, acc_ref):
    qi = pl.program_id(1)
    tq = ATT_BLOCK
    even = (s_even, c_even)
    odd = (s_odd, c_odd)

    m_ref[...] = jnp.full(m_ref.shape, MASK_VALUE, f32)
    acc_ref[...] = jnp.zeros(acc_ref.shape, f32)

    def qk(hh, j, dst, diagonal):
        s_dst, c_dst = dst
        kb = k_ref[0, hh, pl.ds(pl.multiple_of(j * tq, tq), tq), :]
        st = jnp.dot(kb, qt_ref[0, 0, hh], preferred_element_type=f32)
        if diagonal:
            kc = lax.broadcasted_iota(jnp.int32, st.shape, 0) // CHUNK
            qc = lax.broadcasted_iota(jnp.int32, st.shape, 1) // CHUNK
            st = jnp.where(kc <= qc, st, MASK_VALUE)
        s_dst[hh] = st
        c_dst[hh] = jnp.max(st, axis=0, keepdims=True)

    def attend(hh, j, src):
        s_src, c_src = src
        m_prev = m_ref[hh]
        m_new = jnp.maximum(m_prev, c_src[hh])
        alpha = jnp.exp2(m_prev - m_new)
        p = jnp.exp2(s_src[hh] - m_new).astype(bf16)
        upd = jnp.dot(vt_ref[0, j, hh], p, preferred_element_type=f32)
        acc_ref[hh] = alpha * acc_ref[hh] + upd
        m_ref[hh] = m_new

    def step(j, cur, nxt, next_is_diagonal):
        for hh in range(MLA_HEADS):
            qk(hh, j + 1, nxt, next_is_diagonal)
            attend(hh, j, cur)

    def finish(cur):
        heads = []
        for hh in range(MLA_HEADS):
            attend(hh, qi, cur)
            acc = acc_ref[hh]
            heads.append(acc[:V_HEAD_DIM] / acc[V_HEAD_DIM:V_HEAD_DIM + 1])
        y = jnp.concatenate(heads, axis=0).T
        o_ref[0] = (_rms(y, gmla_ref[...]) * gm_ref[0].astype(f32)).astype(bf16)

    @pl.when(qi == 0)
    def _():
        for hh in range(MLA_HEADS):
            qk(hh, 0, even, True)
        finish(even)

    @pl.when(qi > 0)
    def _():
        for hh in range(MLA_HEADS):
            qk(hh, 0, even, False)

    def pair(jj, carry):
        step(2 * jj, even, odd, False)
        step(2 * jj + 1, odd, even, False)
        return carry

    lax.fori_loop(0, lax.shift_right_logical(jnp.maximum(qi - 1, 0), 1), pair, 0)

    @pl.when((qi > 0) & (lax.rem(qi, 2) == 1))
    def _():
        step(qi - 1, even, odd, True)
        finish(odd)

    @pl.when((qi > 0) & (lax.rem(qi, 2) == 0))
    def _():
        step(qi - 2, even, odd, False)
        step(qi - 1, odd, even, True)
        finish(even)


def _out_kernel(ym_ref, yr_ref, x_ref, wout_ref, gfin_ref, o_ref):
    y = jnp.concatenate([ym_ref[0], yr_ref[0]], axis=-1)
    xo = x_ref[0] + jnp.dot(y, wout_ref[...], preferred_element_type=f32)
    o_ref[0] = _rms(xo, gfin_ref[...])


def _rot_half(w):
    half = QK_ROPE_DIM // 2
    return jnp.concatenate([-w[..., half:], w[..., :half]], axis=-1)


def _prepare_weights(w_in, w_q_b, w_kv_b, w_rg_a, w_rg_x):
    offs = [0, 256, 384, 416, 928, 1440, 1952]
    w_qlat, w_kvlat, w_krope, w_gm, w_xr, w_gr = (
        w_in[:, offs[i]:offs[i + 1]] for i in range(6))
    kpe_blk = jnp.concatenate(
        [jnp.zeros((D_MODEL, QK_NOPE_DIM), w_in.dtype), w_krope, _rot_half(w_krope)], axis=1)
    w_in_r = jnp.concatenate([w_qlat, w_kvlat, kpe_blk, w_gm, w_xr, w_gr], axis=1)

    wq = w_q_b.reshape(Q_LORA_RANK, MLA_HEADS, QK_HEAD_DIM)
    wq_nope = wq[:, :, :QK_NOPE_DIM].reshape(Q_LORA_RANK, -1)
    wq_pe = wq[:, :, QK_NOPE_DIM:]
    wq_r = jnp.concatenate([wq_nope, wq_pe.reshape(Q_LORA_RANK, -1),
                            _rot_half(wq_pe).reshape(Q_LORA_RANK, -1)], axis=1)

    wkv = w_kv_b.reshape(KV_LORA_RANK, MLA_HEADS, QK_NOPE_DIM + V_HEAD_DIM)
    wk = jnp.pad(wkv[:, :, :QK_NOPE_DIM], ((0, 0), (0, 0), (0, HEAD_PAD - QK_NOPE_DIM)))
    wk = wk.reshape(KV_LORA_RANK, MLA_HEADS * HEAD_PAD)
    wv = wkv[:, :, QK_NOPE_DIM:].reshape(KV_LORA_RANK, MLA_WIDTH)
    wkv_r = jnp.concatenate([wk, wv], axis=1)

    per = GATE_GROUP // RNN_BLOCK_DIM
    eye = jnp.eye(per, dtype=w_rg_a.dtype)

    def block_diag(w):
        w = w.reshape(RNN_BLOCKS // per, per, RNN_BLOCK_DIM, RNN_BLOCK_DIM)
        return jnp.einsum('gncd,nm->gncmd', w, eye).reshape(-1, GATE_GROUP, GATE_GROUP)

    wgate = jnp.concatenate([block_diag(w_rg_a), block_diag(w_rg_x)], axis=2)
    return (w_in_r.astype(bf16), wq_r.astype(bf16), wkv_r.astype(bf16), wgate.astype(bf16))


@functools.lru_cache(maxsize=None)
def _rope_tables(seq):
    pos = np.arange(seq, dtype=np.float64)
    inv_freq = ROPE_THETA ** (-np.arange(0, QK_ROPE_DIM, 2, dtype=np.float64) / QK_ROPE_DIM)
    ang = pos[:, None] * inv_freq[None, :]
    cos = np.tile(np.cos(ang), (1, 2))
    sin = np.tile(np.sin(ang), (1, 2))
    scale = math.log2(math.e) / math.sqrt(QK_HEAD_DIM)
    cq = np.tile(cos, (1, MLA_HEADS)) * scale
    sq = np.tile(sin, (1, MLA_HEADS)) * scale
    pads = ((0, 0), (QK_NOPE_DIM, HEAD_PAD - QK_HEAD_DIM))
    return tuple(t.astype(np.float32) for t in (cq, sq, np.pad(cos, pads), np.pad(sin, pads)))


def _const_spec(shape):
    return pl.BlockSpec(shape, lambda *_: (0,) * len(shape))


def kernel(x, norm_in_g, w_in, q_norm_g, w_q_b, kv_norm_g, w_kv_b, conv_w, conv_b,
           w_rg_a, b_rg_a, w_rg_x, b_rg_x, lru_param, out_norm_mla_g, out_norm_rnn_g,
           w_out, final_norm_g):
    b, s, _ = x.shape
    ts = SEQ_TILE
    tq = ATT_BLOCK
    assert s % ts == 0 and ts % tq == 0 and tq % CHUNK == 0 and s % OUT_TILE == 0
    assert w_in.shape[0] == 1, "the final norm is fused into the single layer's output kernel"
    cq, sq, ck, sk = _rope_tables(s)
    row = lambda v: v.reshape(1, -1).astype(f32)
    params = pltpu.CompilerParams(
        dimension_semantics=("arbitrary", "arbitrary"), vmem_limit_bytes=VMEM_LIMIT)

    w_in_r, wq, wkv, wgate = _prepare_weights(w_in[0], w_q_b[0], w_kv_b[0], w_rg_a[0], w_rg_x[0])
    bgate = jnp.stack([b_rg_a[0], b_rg_x[0]]).astype(f32)

    seq_spec = lambda w: pl.BlockSpec((1, ts, w), lambda bi, si: (bi, si, 0))
    tab_spec = lambda w: pl.BlockSpec((ts, w), lambda bi, si: (si, 0))
    qt, k, vt, gm, yr = pl.pallas_call(
        _proj_kernel,
        grid=(b, s // ts),
        in_specs=[
            seq_spec(D_MODEL), _const_spec((1, D_MODEL)), _const_spec((D_MODEL, Z_WIDTH)),
            _const_spec((1, Q_LORA_RANK)), _const_spec(wq.shape),
            _const_spec((1, KV_LORA_RANK)), _const_spec(wkv.shape),
            tab_spec(Q_PE_WIDTH), tab_spec(Q_PE_WIDTH), tab_spec(HEAD_PAD), tab_spec(HEAD_PAD),
            _const_spec((CONV_WIDTH, RNN_WIDTH)), _const_spec((1, RNN_WIDTH)),
            _const_spec(wgate.shape), _const_spec((2, RNN_WIDTH)),
            _const_spec((1, RNN_WIDTH)), _const_spec((1, RNN_WIDTH)),
        ],
        out_specs=[
            pl.BlockSpec((1, ts // tq, MLA_HEADS, HEAD_PAD, tq), lambda bi, si: (bi, si, 0, 0, 0)),
            pl.BlockSpec((1, MLA_HEADS, ts, HEAD_PAD), lambda bi, si: (bi, 0, si, 0)),
            pl.BlockSpec((1, ts // tq, MLA_HEADS, V_EXT, tq), lambda bi, si: (bi, si, 0, 0, 0)),
            seq_spec(MLA_WIDTH), seq_spec(RNN_WIDTH),
        ],
        out_shape=[
            jax.ShapeDtypeStruct((b, s // tq, MLA_HEADS, HEAD_PAD, tq), bf16),
            jax.ShapeDtypeStruct((b, MLA_HEADS, s, HEAD_PAD), bf16),
            jax.ShapeDtypeStruct((b, s // tq, MLA_HEADS, V_EXT, tq), bf16),
            jax.ShapeDtypeStruct((b, s, MLA_WIDTH), bf16),
            jax.ShapeDtypeStruct((b, s, RNN_WIDTH), bf16),
        ],
        scratch_shapes=[pltpu.VMEM((CONV_WIDTH - 1, SUBLANES, RNN_WIDTH), f32),
                        pltpu.VMEM((1, RNN_WIDTH), f32),
                        pltpu.VMEM((RNN_WIDTH // LANES, SUBLANES * PERM_PITCH, LANES), f32)],
        compiler_params=params,
        name="proj_rnn",
    )(x, row(norm_in_g[0]), w_in_r, row(q_norm_g[0]), wq, row(kv_norm_g[0]), wkv,
      cq, sq, ck, sk, conv_w[0].astype(f32), row(conv_b[0]), wgate, bgate,
      row(lru_param[0]), row(out_norm_rnn_g[0]))

    ym = pl.pallas_call(
        _attn_kernel,
        grid=(b, s // tq),
        in_specs=[
            pl.BlockSpec((1, 1, MLA_HEADS, HEAD_PAD, tq), lambda bi, qi: (bi, qi, 0, 0, 0)),
            pl.BlockSpec((1, MLA_HEADS, s, HEAD_PAD), lambda bi, qi: (bi, 0, 0, 0)),
            pl.BlockSpec((1, s // tq, MLA_HEADS, V_EXT, tq), lambda bi, qi: (bi, 0, 0, 0, 0)),
            pl.BlockSpec((1, tq, MLA_WIDTH), lambda bi, qi: (bi, qi, 0)),
            _const_spec((1, MLA_WIDTH)),
        ],
        out_specs=pl.BlockSpec((1, tq, MLA_WIDTH), lambda bi, qi: (bi, qi, 0)),
        out_shape=jax.ShapeDtypeStruct((b, s, MLA_WIDTH), bf16),
        scratch_shapes=[pltpu.VMEM((MLA_HEADS, tq, tq), f32),
                        pltpu.VMEM((MLA_HEADS, tq, tq), f32),
                        pltpu.VMEM((MLA_HEADS, 1, tq), f32),
                        pltpu.VMEM((MLA_HEADS, 1, tq), f32),
                        pltpu.VMEM((MLA_HEADS, 1, tq), f32),
                        pltpu.VMEM((MLA_HEADS, V_EXT, tq), f32)],
        compiler_params=params,
        name="mla_attention",
    )(qt, k, vt, gm, row(out_norm_mla_g[0]))

    to = OUT_TILE
    out_spec = lambda w: pl.BlockSpec((1, to, w), lambda bi, si: (bi, si, 0))
    return pl.pallas_call(
        _out_kernel,
        grid=(b, s // to),
        in_specs=[
            out_spec(MLA_WIDTH), out_spec(RNN_WIDTH), out_spec(D_MODEL),
            _const_spec((D_MODEL, D_MODEL)), _const_spec((1, D_MODEL)),
        ],
        out_specs=out_spec(D_MODEL),
        out_shape=jax.ShapeDtypeStruct((b, s, D_MODEL), f32),
        compiler_params=params,
        name="out_proj",
    )(ym, yr, x, w_out[0].astype(bf16), row(final_norm_g))
```

```python
import functools
import math

import jax
import jax.numpy as jnp
import numpy as np
from jax import lax
from jax.experimental import pallas as pl
from jax.experimental.pallas import tpu as pltpu

D_MODEL = 1024
CHUNK = 64
MLA_HEADS = 8
QK_NOPE_DIM = 64
QK_ROPE_DIM = 32
QK_HEAD_DIM = QK_NOPE_DIM + QK_ROPE_DIM
V_HEAD_DIM = 64
MLA_WIDTH = MLA_HEADS * V_HEAD_DIM
Q_LORA_RANK = 256
KV_LORA_RANK = 128
ROPE_THETA = 10000.0
RNN_WIDTH = 512
RNN_BLOCKS = 8
RNN_BLOCK_DIM = RNN_WIDTH // RNN_BLOCKS
CONV_WIDTH = 4
LRU_C = 8.0
NORM_EPS = 1e-6

LANES = 128
SUBLANES = 8
MXU_DIM = 256
HEAD_PAD = LANES
V_EXT = V_HEAD_DIM + 16
Z_WIDTH = 2048
SEQ_TILE = 512
OUT_TILE = 1024
K_BLOCK = MXU_DIM
Q_BLOCK = 2 * MXU_DIM
VMEM_LIMIT = 56 * 1024 * 1024

Z_Q, Z_KV, Z_KPE, Z_GM, Z_XR, Z_GR = 0, 256, 384, 512, 1024, 1536
Q_NOPE, Q_PE, Q_ROT = 0, MLA_HEADS * QK_NOPE_DIM, MLA_HEADS * (QK_NOPE_DIM + QK_ROPE_DIM)
Q_PE_WIDTH = MLA_HEADS * QK_ROPE_DIM
GATE_GROUP = MXU_DIM
PERM_PITCH = SEQ_TILE // SUBLANES + SUBLANES

MASK_VALUE = -0.5 * float(jnp.finfo(jnp.float32).max)

f32 = jnp.float32
bf16 = jnp.bfloat16


def _rms(x, g):
    ms = jnp.mean(x * x, axis=-1, keepdims=True)
    return x * lax.rsqrt(ms + NORM_EPS) * g


def _sigmoid(x):
    return 0.5 * jnp.tanh(0.5 * x) + 0.5


def _silu(x):
    hx = 0.5 * x
    return hx * jnp.tanh(hx) + hx


def _to_lockstep(x, buf):
    rows = x.shape[0]
    seg = rows // SUBLANES
    slabs = x.shape[1] // LANES
    for c in range(slabs):
        for r in range(SUBLANES):
            buf[c, r * PERM_PITCH:r * PERM_PITCH + seg, :] = (
                x[r * seg:(r + 1) * seg, c * LANES:(c + 1) * LANES])
    return [jnp.concatenate([buf[c, pl.ds(i, SUBLANES, stride=PERM_PITCH), :]
                             for c in range(slabs)], axis=1) for i in range(seg)]


def _from_lockstep(pieces, buf):
    seg = len(pieces)
    slabs = pieces[0].shape[1] // LANES
    for i, piece in enumerate(pieces):
        for c in range(slabs):
            buf[c, pl.ds(i, SUBLANES, stride=PERM_PITCH), :] = piece[:, c * LANES:(c + 1) * LANES]
    return jnp.concatenate(
        [jnp.concatenate([buf[c, r * PERM_PITCH:r * PERM_PITCH + seg, :] for c in range(slabs)],
                         axis=1) for r in range(SUBLANES)], axis=0)


def _proj_kernel(x_ref, gin_ref, win_ref, qg_ref, wq_ref, kvg_ref, wkv_ref,
                 cq_ref, sq_ref, ck_ref, sk_ref, convw_ref, convb_ref, wgate_ref,
                 bgate_ref, lru_ref, grnn_ref,
                 qt_out, k_out, vt_out, gm_out, yr_out,
                 xr_tail, h_carry, perm_buf):
    s = pl.program_id(1)
    ts = x_ref.shape[1]

    @pl.when(s == 0)
    def _():
        xr_tail[...] = jnp.zeros(xr_tail.shape, f32)
        h_carry[...] = jnp.zeros(h_carry.shape, f32)

    h = _rms(x_ref[0], gin_ref[...]).astype(bf16)

    def in_proj(lo, width):
        return jnp.dot(h, win_ref[:, lo:lo + width], preferred_element_type=f32)

    z_xr = in_proj(Z_XR, RNN_WIDTH)
    z_a = in_proj(0, Z_GM)
    xb, xr = _conv_lockstep(z_xr, xr_tail, perm_buf, convw_ref, convb_ref)
    gates = [jnp.dot(xb[:, g * GATE_GROUP:(g + 1) * GATE_GROUP], wgate_ref[g],
                     preferred_element_type=f32)
             for g in range(RNN_WIDTH // GATE_GROUP)]
    z_gm = in_proj(Z_GM, MLA_WIDTH)
    qn = _rms(z_a[:, Z_Q:Z_Q + Q_LORA_RANK], qg_ref[...]).astype(bf16)
    q = jnp.dot(qn, wq_ref[...], preferred_element_type=f32)
    kvn = _rms(z_a[:, Z_KV:Z_KV + KV_LORA_RANK], kvg_ref[...]).astype(bf16)
    kv = jnp.dot(kvn, wkv_ref[...], preferred_element_type=f32)
    z_gr = in_proj(Z_GR, RNN_WIDTH)

    y = _rg_lru(s, gates, xr, bgate_ref, lru_ref, h_carry, perm_buf)
    yr_out[0] = (_rms(y, grnn_ref[...]) * _silu(z_gr)).astype(bf16)

    scale = math.log2(math.e) / math.sqrt(QK_HEAD_DIM)
    q_nope_t = (q[:, Q_NOPE:Q_PE] * scale).T
    q_pe = q[:, Q_PE:Q_ROT] * cq_ref[...] + q[:, Q_ROT:] * sq_ref[...]
    q_pe_t = q_pe.T
    pad = jnp.zeros((HEAD_PAD - QK_HEAD_DIM, Q_BLOCK), bf16)
    for j in range(ts // Q_BLOCK):
        cols = slice(j * Q_BLOCK, (j + 1) * Q_BLOCK)
        for hh in range(MLA_HEADS):
            qt_out[0, j, hh, :QK_NOPE_DIM, :] = (
                q_nope_t[hh * QK_NOPE_DIM:(hh + 1) * QK_NOPE_DIM, cols].astype(bf16))
            qt_out[0, j, hh, QK_NOPE_DIM:QK_HEAD_DIM, :] = (
                q_pe_t[hh * QK_ROPE_DIM:(hh + 1) * QK_ROPE_DIM, cols].astype(bf16))
            qt_out[0, j, hh, QK_HEAD_DIM:, :] = pad

    kblk = z_a[:, Z_KPE:Z_KPE + HEAD_PAD]
    kpe = kblk * ck_ref[...] + pltpu.roll(kblk, LANES - QK_ROPE_DIM, 1) * sk_ref[...]
    for hh in range(MLA_HEADS):
        k_out[0, hh] = (kv[:, hh * HEAD_PAD:(hh + 1) * HEAD_PAD] + kpe).astype(bf16)
    vt = kv[:, MLA_HEADS * HEAD_PAD:].T
    ones_row = lax.broadcasted_iota(jnp.int32, (MLA_HEADS, V_EXT - V_HEAD_DIM, K_BLOCK), 1) == 0
    for j in range(ts // K_BLOCK):
        blk = vt[:, j * K_BLOCK:(j + 1) * K_BLOCK]
        vt_out[0, j, :, :V_HEAD_DIM, :] = (
            blk.reshape(MLA_HEADS, V_HEAD_DIM, K_BLOCK).astype(bf16))
        vt_out[0, j, :, V_HEAD_DIM:, :] = jnp.where(ones_row, 1.0, 0.0).astype(bf16)

    gm_out[0] = _silu(z_gm).astype(bf16)


def _conv_lockstep(x_rnn, xr_tail, perm_buf, convw_ref, convb_ref):
    ts = x_rnn.shape[0]
    sub8 = lax.broadcasted_iota(jnp.int32, (SUBLANES, RNN_WIDTH), 0)
    xl = _to_lockstep(x_rnn, perm_buf)
    hist = []
    for k in range(CONV_WIDTH - 1):
        prev = xr_tail[k]
        cur = xl[len(xl) - (CONV_WIDTH - 1) + k]
        hist.append(pltpu.roll(jnp.where(sub8 == SUBLANES - 1, prev, cur), 1, 0))
        xr_tail[k] = cur
    xext = jnp.concatenate(hist + xl, axis=0)
    xr = convb_ref[...]
    for k in range(CONV_WIDTH):
        xr = xr + convw_ref[k:k + 1, :] * xext[k * SUBLANES:k * SUBLANES + ts]
    return xr.astype(bf16), xr


def _rg_lru(s, gates, xr, bgate_ref, lru_ref, h_carry, perm_buf):
    ts = xr.shape[0]
    sub8 = lax.broadcasted_iota(jnp.int32, (SUBLANES, RNN_WIDTH), 0)
    ga = [gg[:, :GATE_GROUP] for gg in gates]
    gx = [gg[:, GATE_GROUP:] for gg in gates]
    r = _sigmoid(jnp.concatenate(ga, axis=1) + bgate_ref[0:1, :])
    i = _sigmoid(jnp.concatenate(gx, axis=1) + bgate_ref[1:2, :])
    nl = -lru_ref[...]
    softplus = jnp.maximum(nl, 0.0) + jnp.log1p(jnp.exp(-jnp.abs(nl)))
    a = jnp.exp2(r * ((-LRU_C * math.log2(math.e)) * softplus))
    m2 = jnp.maximum(1.0 - a * a, 1e-12)
    mult = m2 * lax.rsqrt(m2)
    row = lax.broadcasted_iota(jnp.int32, (ts, RNN_WIDTH), 0)
    mult = jnp.where((row == 0) & (s == 0), 1.0, mult)
    u = mult * (i * xr)

    steps = ts // SUBLANES
    hl, pr = [], []
    hcur = pcur = None
    for t in range(steps):
        at = a[t * SUBLANES:(t + 1) * SUBLANES]
        ut = u[t * SUBLANES:(t + 1) * SUBLANES]
        hcur = ut if t == 0 else at * hcur + ut
        pcur = at if t == 0 else at * pcur
        hl.append(hcur)
        pr.append(pcur)
    pe, he = pcur, hcur
    for d in (1, 2, 4):
        keep = sub8 >= d
        he = jnp.where(keep, pe * pltpu.roll(he, d, 0) + he, he)
        pe = jnp.where(keep, pe * pltpu.roll(pe, d, 0), pe)
    carry = h_carry[...]
    ends = he + pe * carry
    h_carry[...] = ends[SUBLANES - 1:SUBLANES, :]
    init = jnp.where(sub8 == 0, carry, pltpu.roll(ends, 1, 0))
    return _from_lockstep([hl[t] + pr[t] * init for t in range(steps)], perm_buf)


def _attn_kernel(qt_ref, k_ref, vt_ref, gm_ref, gmla_ref, o_ref,
                 s_even, s_odd, c_even, c_odd, m_ref, acc_ref):
    qi = pl.program_id(1)
    tk = K_BLOCK
    per_q = Q_BLOCK // K_BLOCK
    first_own = per_q * qi
    even = (s_even, c_even)
    odd = (s_odd, c_odd)

    m_ref[...] = jnp.full(m_ref.shape, MASK_VALUE, f32)
    acc_ref[...] = jnp.zeros(acc_ref.shape, f32)

    def qk(hh, j, dst, own):
        s_dst, c_dst = dst
        kb = k_ref[0, hh, pl.ds(pl.multiple_of(j * tk, tk), tk), :]
        st = jnp.dot(kb, qt_ref[0, 0, hh], preferred_element_type=f32)
        if own is not None:
            kc = (lax.broadcasted_iota(jnp.int32, st.shape, 0) + own * tk) // CHUNK
            qc = lax.broadcasted_iota(jnp.int32, st.shape, 1) // CHUNK
            st = jnp.where(kc <= qc, st, MASK_VALUE)
        s_dst[hh] = st
        c_dst[hh] = jnp.max(st, axis=0, keepdims=True)

    def attend(hh, j, src):
        s_src, c_src = src
        m_prev = m_ref[hh]
        m_new = jnp.maximum(m_prev, c_src[hh])
        alpha = jnp.exp2(m_prev - m_new)
        p = jnp.exp2(s_src[hh] - m_new).astype(bf16)
        upd = jnp.dot(vt_ref[0, j, hh], p, preferred_element_type=f32)
        acc_ref[hh] = alpha * acc_ref[hh] + upd
        m_ref[hh] = m_new

    def step(j, cur, nxt, next_own):
        for hh in range(MLA_HEADS):
            qk(hh, j + 1, nxt, next_own)
            attend(hh, j, cur)

    def finish(j, cur):
        heads = []
        for hh in range(MLA_HEADS):
            attend(hh, j, cur)
            acc = acc_ref[hh]
            heads.append(acc[:V_HEAD_DIM] / acc[V_HEAD_DIM:V_HEAD_DIM + 1])
        y = jnp.concatenate(heads, axis=0).T
        o_ref[0] = (_rms(y, gmla_ref[...]) * gm_ref[0].astype(f32)).astype(bf16)

    @pl.when(qi == 0)
    def _():
        for hh in range(MLA_HEADS):
            qk(hh, 0, even, 0)
        step(0, even, odd, 1)
        finish(1, odd)

    @pl.when(qi > 0)
    def _():
        for hh in range(MLA_HEADS):
            qk(hh, 0, even, None)

    def pair(jj, carry):
        step(2 * jj, even, odd, None)
        step(2 * jj + 1, odd, even, None)
        return carry

    lax.fori_loop(0, jnp.maximum(qi - 1, 0), pair, 0)

    @pl.when(qi > 0)
    def _():
        step(first_own - 2, even, odd, None)
        step(first_own - 1, odd, even, 0)
        step(first_own, even, odd, 1)
        finish(first_own + 1, odd)


def _out_kernel(ym_ref, yr_ref, x_ref, wout_ref, gfin_ref, o_ref):
    y = jnp.concatenate([ym_ref[0], yr_ref[0]], axis=-1)
    xo = x_ref[0] + jnp.dot(y, wout_ref[...], preferred_element_type=f32)
    o_ref[0] = _rms(xo, gfin_ref[...])


def _rot_half(w):
    half = QK_ROPE_DIM // 2
    return jnp.concatenate([-w[..., half:], w[..., :half]], axis=-1)


def _prepare_weights(w_in, w_q_b, w_kv_b, w_rg_a, w_rg_x):
    offs = [0, 256, 384, 416, 928, 1440, 1952]
    w_qlat, w_kvlat, w_krope, w_gm, w_xr, w_gr = (
        w_in[:, offs[i]:offs[i + 1]] for i in range(6))
    kpe_blk = jnp.concatenate(
        [jnp.zeros((D_MODEL, QK_NOPE_DIM), w_in.dtype), w_krope, _rot_half(w_krope)], axis=1)
    w_in_r = jnp.concatenate([w_qlat, w_kvlat, kpe_blk, w_gm, w_xr, w_gr], axis=1)

    wq = w_q_b.reshape(Q_LORA_RANK, MLA_HEADS, QK_HEAD_DIM)
    wq_nope = wq[:, :, :QK_NOPE_DIM].reshape(Q_LORA_RANK, -1)
    wq_pe = wq[:, :, QK_NOPE_DIM:]
    wq_r = jnp.concatenate([wq_nope, wq_pe.reshape(Q_LORA_RANK, -1),
                            _rot_half(wq_pe).reshape(Q_LORA_RANK, -1)], axis=1)

    wkv = w_kv_b.reshape(KV_LORA_RANK, MLA_HEADS, QK_NOPE_DIM + V_HEAD_DIM)
    wk = jnp.pad(wkv[:, :, :QK_NOPE_DIM], ((0, 0), (0, 0), (0, HEAD_PAD - QK_NOPE_DIM)))
    wk = wk.reshape(KV_LORA_RANK, MLA_HEADS * HEAD_PAD)
    wv = wkv[:, :, QK_NOPE_DIM:].reshape(KV_LORA_RANK, MLA_WIDTH)
    wkv_r = jnp.concatenate([wk, wv], axis=1)

    per = GATE_GROUP // RNN_BLOCK_DIM
    eye = jnp.eye(per, dtype=w_rg_a.dtype)

    def block_diag(w):
        w = w.reshape(RNN_BLOCKS // per, per, RNN_BLOCK_DIM, RNN_BLOCK_DIM)
        return jnp.einsum('gncd,nm->gncmd', w, eye).reshape(-1, GATE_GROUP, GATE_GROUP)

    wgate = jnp.concatenate([block_diag(w_rg_a), block_diag(w_rg_x)], axis=2)
    return (w_in_r.astype(bf16), wq_r.astype(bf16), wkv_r.astype(bf16), wgate.astype(bf16))


@functools.lru_cache(maxsize=None)
def _rope_tables(seq):
    pos = np.arange(seq, dtype=np.float64)
    inv_freq = ROPE_THETA ** (-np.arange(0, QK_ROPE_DIM, 2, dtype=np.float64) / QK_ROPE_DIM)
    ang = pos[:, None] * inv_freq[None, :]
    cos = np.tile(np.cos(ang), (1, 2))
    sin = np.tile(np.sin(ang), (1, 2))
    scale = math.log2(math.e) / math.sqrt(QK_HEAD_DIM)
    cq = np.tile(cos, (1, MLA_HEADS)) * scale
    sq = np.tile(sin, (1, MLA_HEADS)) * scale
    pads = ((0, 0), (QK_NOPE_DIM, HEAD_PAD - QK_HEAD_DIM))
    return tuple(t.astype(np.float32) for t in (cq, sq, np.pad(cos, pads), np.pad(sin, pads)))


def _const_spec(shape):
    return pl.BlockSpec(shape, lambda *_: (0,) * len(shape))


def kernel(x, norm_in_g, w_in, q_norm_g, w_q_b, kv_norm_g, w_kv_b, conv_w, conv_b,
           w_rg_a, b_rg_a, w_rg_x, b_rg_x, lru_param, out_norm_mla_g, out_norm_rnn_g,
           w_out, final_norm_g):
    b, s, _ = x.shape
    ts = SEQ_TILE
    tq, tk = Q_BLOCK, K_BLOCK
    assert s % ts == 0 and ts % tq == 0 and tq % tk == 0 and tk % CHUNK == 0
    assert s % OUT_TILE == 0
    assert w_in.shape[0] == 1, "the final norm is fused into the single layer's output kernel"
    cq, sq, ck, sk = _rope_tables(s)
    row = lambda v: v.reshape(1, -1).astype(f32)
    params = pltpu.CompilerParams(
        dimension_semantics=("arbitrary", "arbitrary"), vmem_limit_bytes=VMEM_LIMIT)

    w_in_r, wq, wkv, wgate = _prepare_weights(w_in[0], w_q_b[0], w_kv_b[0], w_rg_a[0], w_rg_x[0])
    bgate = jnp.stack([b_rg_a[0], b_rg_x[0]]).astype(f32)

    seq_spec = lambda w: pl.BlockSpec((1, ts, w), lambda bi, si: (bi, si, 0))
    tab_spec = lambda w: pl.BlockSpec((ts, w), lambda bi, si: (si, 0))
    qt, k, vt, gm, yr = pl.pallas_call(
        _proj_kernel,
        grid=(b, s // ts),
        in_specs=[
            seq_spec(D_MODEL), _const_spec((1, D_MODEL)), _const_spec((D_MODEL, Z_WIDTH)),
            _const_spec((1, Q_LORA_RANK)), _const_spec(wq.shape),
            _const_spec((1, KV_LORA_RANK)), _const_spec(wkv.shape),
            tab_spec(Q_PE_WIDTH), tab_spec(Q_PE_WIDTH), tab_spec(HEAD_PAD), tab_spec(HEAD_PAD),
            _const_spec((CONV_WIDTH, RNN_WIDTH)), _const_spec((1, RNN_WIDTH)),
            _const_spec(wgate.shape), _const_spec((2, RNN_WIDTH)),
            _const_spec((1, RNN_WIDTH)), _const_spec((1, RNN_WIDTH)),
        ],
        out_specs=[
            pl.BlockSpec((1, ts // tq, MLA_HEADS, HEAD_PAD, tq), lambda bi, si: (bi, si, 0, 0, 0)),
            pl.BlockSpec((1, MLA_HEADS, ts, HEAD_PAD), lambda bi, si: (bi, 0, si, 0)),
            pl.BlockSpec((1, ts // tk, MLA_HEADS, V_EXT, tk), lambda bi, si: (bi, si, 0, 0, 0)),
            seq_spec(MLA_WIDTH), seq_spec(RNN_WIDTH),
        ],
        out_shape=[
            jax.ShapeDtypeStruct((b, s // tq, MLA_HEADS, HEAD_PAD, tq), bf16),
            jax.ShapeDtypeStruct((b, MLA_HEADS, s, HEAD_PAD), bf16),
            jax.ShapeDtypeStruct((b, s // tk, MLA_HEADS, V_EXT, tk), bf16),
            jax.ShapeDtypeStruct((b, s, MLA_WIDTH), bf16),
            jax.ShapeDtypeStruct((b, s, RNN_WIDTH), bf16),
        ],
        scratch_shapes=[pltpu.VMEM((CONV_WIDTH - 1, SUBLANES, RNN_WIDTH), f32),
                        pltpu.VMEM((1, RNN_WIDTH), f32),
                        pltpu.VMEM((RNN_WIDTH // LANES, SUBLANES * PERM_PITCH, LANES), f32)],
        compiler_params=params,
        name="proj_rnn",
    )(x, row(norm_in_g[0]), w_in_r, row(q_norm_g[0]), wq, row(kv_norm_g[0]), wkv,
      cq, sq, ck, sk, conv_w[0].astype(f32), row(conv_b[0]), wgate, bgate,
      row(lru_param[0]), row(out_norm_rnn_g[0]))

    ym = pl.pallas_call(
        _attn_kernel,
        grid=(b, s // tq),
        in_specs=[
            pl.BlockSpec((1, 1, MLA_HEADS, HEAD_PAD, tq), lambda bi, qi: (bi, qi, 0, 0, 0)),
            pl.BlockSpec((1, MLA_HEADS, s, HEAD_PAD), lambda bi, qi: (bi, 0, 0, 0)),
            pl.BlockSpec((1, s // tk, MLA_HEADS, V_EXT, tk), lambda bi, qi: (bi, 0, 0, 0, 0)),
            pl.BlockSpec((1, tq, MLA_WIDTH), lambda bi, qi: (bi, qi, 0)),
            _const_spec((1, MLA_WIDTH)),
        ],
        out_specs=pl.BlockSpec((1, tq, MLA_WIDTH), lambda bi, qi: (bi, qi, 0)),
        out_shape=jax.ShapeDtypeStruct((b, s, MLA_WIDTH), bf16),
        scratch_shapes=[pltpu.VMEM((MLA_HEADS, tk, tq), f32),
                        pltpu.VMEM((MLA_HEADS, tk, tq), f32),
                        pltpu.VMEM((MLA_HEADS, 1, tq), f32),
                        pltpu.VMEM((MLA_HEADS, 1, tq), f32),
                        pltpu.VMEM((MLA_HEADS, 1, tq), f32),
                        pltpu.VMEM((MLA_HEADS, V_EXT, tq), f32)],
        compiler_params=params,
        name="mla_attention",
    )(qt, k, vt, gm, row(out_norm_mla_g[0]))

    to = OUT_TILE
    out_spec = lambda w: pl.BlockSpec((1, to, w), lambda bi, si: (bi, si, 0))
    return pl.pallas_call(
        _out_kernel,
        grid=(b, s // to),
        in_specs=[
            out_spec(MLA_WIDTH), out_spec(RNN_WIDTH), out_spec(D_MODEL),
            _const_spec((D_MODEL, D_MODEL)), _const_spec((1, D_MODEL)),
        ],
        out_specs=out_spec(D_MODEL),
        out_shape=jax.ShapeDtypeStruct((b, s, D_MODEL), f32),
        compiler_params=params,
        name="out_proj",
    )(ym, yr, x, w_out[0].astype(bf16), row(final_norm_g))
```

```python
import functools
import math

import jax
import jax.numpy as jnp
import numpy as np
from jax import lax
from jax.experimental import pallas as pl
from jax.experimental.pallas import tpu as pltpu

D_MODEL = 1024
CHUNK = 64
MLA_HEADS = 8
QK_NOPE_DIM = 64
QK_ROPE_DIM = 32
QK_HEAD_DIM = QK_NOPE_DIM + QK_ROPE_DIM
V_HEAD_DIM = 64
MLA_WIDTH = MLA_HEADS * V_HEAD_DIM
Q_LORA_RANK = 256
KV_LORA_RANK = 128
ROPE_THETA = 10000.0
RNN_WIDTH = 512
RNN_BLOCKS = 8
RNN_BLOCK_DIM = RNN_WIDTH // RNN_BLOCKS
CONV_WIDTH = 4
LRU_C = 8.0
NORM_EPS = 1e-6

LANES = 128
SUBLANES = 8
MXU_DIM = 256
HEAD_PAD = LANES
V_EXT = V_HEAD_DIM + 16
Z_WIDTH = 2048
SEQ_TILE = 512
OUT_TILE = 1024
K_BLOCK = MXU_DIM
Q_BLOCK = 2 * MXU_DIM
VMEM_LIMIT = 56 * 1024 * 1024

Z_Q, Z_KV, Z_KPE, Z_GM, Z_XR, Z_GR = 0, 256, 384, 512, 1024, 1536
Q_NOPE, Q_PE, Q_ROT = 0, MLA_HEADS * QK_NOPE_DIM, MLA_HEADS * (QK_NOPE_DIM + QK_ROPE_DIM)
Q_PE_WIDTH = MLA_HEADS * QK_ROPE_DIM
GATE_GROUP = MXU_DIM
PERM_PITCH = SEQ_TILE // SUBLANES + SUBLANES

MASK_VALUE = -0.5 * float(jnp.finfo(jnp.float32).max)

f32 = jnp.float32
bf16 = jnp.bfloat16


def _rms(x, g):
    ms = jnp.mean(x * x, axis=-1, keepdims=True)
    return x * lax.rsqrt(ms + NORM_EPS) * g


def _sigmoid(x):
    return 0.5 * jnp.tanh(0.5 * x) + 0.5


def _silu(x):
    hx = 0.5 * x
    return hx * jnp.tanh(hx) + hx


def _to_lockstep(x, buf):
    rows = x.shape[0]
    seg = rows // SUBLANES
    slabs = x.shape[1] // LANES
    for c in range(slabs):
        for r in range(SUBLANES):
            buf[c, r * PERM_PITCH:r * PERM_PITCH + seg, :] = (
                x[r * seg:(r + 1) * seg, c * LANES:(c + 1) * LANES])
    return [jnp.concatenate([buf[c, pl.ds(i, SUBLANES, stride=PERM_PITCH), :]
                             for c in range(slabs)], axis=1) for i in range(seg)]


def _from_lockstep(pieces, buf):
    seg = len(pieces)
    slabs = pieces[0].shape[1] // LANES
    for i, piece in enumerate(pieces):
        for c in range(slabs):
            buf[c, pl.ds(i, SUBLANES, stride=PERM_PITCH), :] = piece[:, c * LANES:(c + 1) * LANES]
    return jnp.concatenate(
        [jnp.concatenate([buf[c, r * PERM_PITCH:r * PERM_PITCH + seg, :] for c in range(slabs)],
                         axis=1) for r in range(SUBLANES)], axis=0)


def _proj_kernel(x_ref, gin_ref, win_ref, qg_ref, wq_ref, kvg_ref, wkv_ref,
                 cq_ref, sq_ref, ck_ref, sk_ref, convw_ref, convb_ref, wgate_ref,
                 bgate_ref, lru_ref, grnn_ref,
                 qt_out, k_out, vt_out, gm_out, yr_out,
                 xr_tail, h_carry, perm_buf):
    s = pl.program_id(1)
    ts = x_ref.shape[1]

    @pl.when(s == 0)
    def _():
        xr_tail[...] = jnp.zeros(xr_tail.shape, f32)
        h_carry[...] = jnp.zeros(h_carry.shape, f32)

    h = _rms(x_ref[0], gin_ref[...]).astype(bf16)

    def in_proj(lo, width):
        return jnp.dot(h, win_ref[:, lo:lo + width], preferred_element_type=f32)

    z_xr = in_proj(Z_XR, RNN_WIDTH)
    z_a = in_proj(0, Z_GM)
    xb, xr = _conv_lockstep(z_xr, xr_tail, perm_buf, convw_ref, convb_ref)
    gates = [jnp.dot(xb[:, g * GATE_GROUP:(g + 1) * GATE_GROUP], wgate_ref[g],
                     preferred_element_type=f32)
             for g in range(RNN_WIDTH // GATE_GROUP)]
    z_gm = in_proj(Z_GM, MLA_WIDTH)
    qn = _rms(z_a[:, Z_Q:Z_Q + Q_LORA_RANK], qg_ref[...]).astype(bf16)
    q = jnp.dot(qn, wq_ref[...], preferred_element_type=f32)
    kvn = _rms(z_a[:, Z_KV:Z_KV + KV_LORA_RANK], kvg_ref[...]).astype(bf16)
    kv = jnp.dot(kvn, wkv_ref[...], preferred_element_type=f32)
    z_gr = in_proj(Z_GR, RNN_WIDTH)

    y = _rg_lru(s, gates, xr, bgate_ref, lru_ref, h_carry, perm_buf)
    yr_out[0] = (_rms(y, grnn_ref[...]) * _silu(z_gr)).astype(bf16)

    scale = math.log2(math.e) / math.sqrt(QK_HEAD_DIM)
    q_nope_t = (q[:, Q_NOPE:Q_PE] * scale).T
    q_pe = q[:, Q_PE:Q_ROT] * cq_ref[...] + q[:, Q_ROT:] * sq_ref[...]
    q_pe_t = q_pe.T
    pad = jnp.zeros((HEAD_PAD - QK_HEAD_DIM, Q_BLOCK), bf16)
    for j in range(ts // Q_BLOCK):
        cols = slice(j * Q_BLOCK, (j + 1) * Q_BLOCK)
        for hh in range(MLA_HEADS):
            qt_out[0, j, hh, :QK_NOPE_DIM, :] = (
                q_nope_t[hh * QK_NOPE_DIM:(hh + 1) * QK_NOPE_DIM, cols].astype(bf16))
            qt_out[0, j, hh, QK_NOPE_DIM:QK_HEAD_DIM, :] = (
                q_pe_t[hh * QK_ROPE_DIM:(hh + 1) * QK_ROPE_DIM, cols].astype(bf16))
            qt_out[0, j, hh, QK_HEAD_DIM:, :] = pad

    kblk = z_a[:, Z_KPE:Z_KPE + HEAD_PAD]
    kpe = kblk * ck_ref[...] + pltpu.roll(kblk, LANES - QK_ROPE_DIM, 1) * sk_ref[...]
    for hh in range(MLA_HEADS):
        k_out[0, hh] = (kv[:, hh * HEAD_PAD:(hh + 1) * HEAD_PAD] + kpe).astype(bf16)
    vt = kv[:, MLA_HEADS * HEAD_PAD:].T
    ones_row = lax.broadcasted_iota(jnp.int32, (MLA_HEADS, V_EXT - V_HEAD_DIM, K_BLOCK), 1) == 0
    for j in range(ts // K_BLOCK):
        blk = vt[:, j * K_BLOCK:(j + 1) * K_BLOCK]
        vt_out[0, j, :, :V_HEAD_DIM, :] = (
            blk.reshape(MLA_HEADS, V_HEAD_DIM, K_BLOCK).astype(bf16))
        vt_out[0, j, :, V_HEAD_DIM:, :] = jnp.where(ones_row, 1.0, 0.0).astype(bf16)

    gm_out[0] = _silu(z_gm).astype(bf16)


def _conv_lockstep(x_rnn, xr_tail, perm_buf, convw_ref, convb_ref):
    ts = x_rnn.shape[0]
    sub8 = lax.broadcasted_iota(jnp.int32, (SUBLANES, RNN_WIDTH), 0)
    xl = _to_lockstep(x_rnn, perm_buf)
    hist = []
    for k in range(CONV_WIDTH - 1):
        prev = xr_tail[k]
        cur = xl[len(xl) - (CONV_WIDTH - 1) + k]
        hist.append(pltpu.roll(jnp.where(sub8 == SUBLANES - 1, prev, cur), 1, 0))
        xr_tail[k] = cur
    xext = jnp.concatenate(hist + xl, axis=0)
    xr = convb_ref[...]
    for k in range(CONV_WIDTH):
        xr = xr + convw_ref[k:k + 1, :] * xext[k * SUBLANES:k * SUBLANES + ts]
    return xr.astype(bf16), xr


def _rg_lru(s, gates, xr, bgate_ref, lru_ref, h_carry, perm_buf):
    ts = xr.shape[0]
    sub8 = lax.broadcasted_iota(jnp.int32, (SUBLANES, RNN_WIDTH), 0)
    ga = [gg[:, :GATE_GROUP] for gg in gates]
    gx = [gg[:, GATE_GROUP:] for gg in gates]
    r = _sigmoid(jnp.concatenate(ga, axis=1) + bgate_ref[0:1, :])
    i = _sigmoid(jnp.concatenate(gx, axis=1) + bgate_ref[1:2, :])
    nl = -lru_ref[...]
    softplus = jnp.maximum(nl, 0.0) + jnp.log1p(jnp.exp(-jnp.abs(nl)))
    a = jnp.exp2(r * ((-LRU_C * math.log2(math.e)) * softplus))
    m2 = jnp.maximum(1.0 - a * a, 1e-12)
    mult = m2 * lax.rsqrt(m2)
    row = lax.broadcasted_iota(jnp.int32, (ts, RNN_WIDTH), 0)
    mult = jnp.where((row == 0) & (s == 0), 1.0, mult)
    u = mult * (i * xr)

    steps = ts // SUBLANES
    hl, pr = [], []
    hcur = pcur = None
    for t in range(steps):
        at = a[t * SUBLANES:(t + 1) * SUBLANES]
        ut = u[t * SUBLANES:(t + 1) * SUBLANES]
        hcur = ut if t == 0 else at * hcur + ut
        pcur = at if t == 0 else at * pcur
        hl.append(hcur)
        pr.append(pcur)
    pe, he = pcur, hcur
    for d in (1, 2, 4):
        keep = sub8 >= d
        he = jnp.where(keep, pe * pltpu.roll(he, d, 0) + he, he)
        pe = jnp.where(keep, pe * pltpu.roll(pe, d, 0), pe)
    carry = h_carry[...]
    ends = he + pe * carry
    h_carry[...] = ends[SUBLANES - 1:SUBLANES, :]
    init = jnp.where(sub8 == 0, carry, pltpu.roll(ends, 1, 0))
    return _from_lockstep([hl[t] + pr[t] * init for t in range(steps)], perm_buf)


def _attn_kernel(qt_ref, k_ref, vt_ref, gm_ref, gmla_ref, o_ref,
                 s_even, s_odd, c_even, c_odd, m_ref, acc_ref):
    qi = pl.program_id(1)
    tk = K_BLOCK
    per_q = Q_BLOCK // K_BLOCK
    first_own = per_q * qi
    even = (s_even, c_even)
    odd = (s_odd, c_odd)

    m_ref[...] = jnp.full(m_ref.shape, MASK_VALUE, f32)
    acc_ref[...] = jnp.zeros(acc_ref.shape, f32)

    def qk(hh, j, dst, own):
        s_dst, c_dst = dst
        kb = k_ref[0, hh, pl.ds(pl.multiple_of(j * tk, tk), tk), :]
        st = jnp.dot(kb, qt_ref[0, 0, hh], preferred_element_type=f32)
        if own is not None:
            kc = (lax.broadcasted_iota(jnp.int32, st.shape, 0) + own * tk) // CHUNK
            qc = lax.broadcasted_iota(jnp.int32, st.shape, 1) // CHUNK
            st = jnp.where(kc <= qc, st, MASK_VALUE)
        s_dst[hh] = st
        c_dst[hh] = jnp.max(st, axis=0, keepdims=True)

    def attend(hh, j, src):
        s_src, c_src = src
        m_prev = m_ref[hh]
        m_new = jnp.maximum(m_prev, c_src[hh])
        alpha = jnp.exp2(m_prev - m_new)
        p = jnp.exp2(s_src[hh] - m_new).astype(bf16)
        upd = jnp.dot(vt_ref[0, j, hh], p, preferred_element_type=f32)
        acc_ref[hh] = alpha * acc_ref[hh] + upd
        m_ref[hh] = m_new

    def step(j, cur, nxt, next_own):
        for hh in range(MLA_HEADS):
            qk(hh, j + 1, nxt, next_own)
            attend(hh, j, cur)

    def finish(j, cur):
        heads = []
        for hh in range(MLA_HEADS):
            attend(hh, j, cur)
            acc = acc_ref[hh]
            heads.append(acc[:V_HEAD_DIM] / acc[V_HEAD_DIM:V_HEAD_DIM + 1])
        y = jnp.concatenate(heads, axis=0).T
        o_ref[0] = (_rms(y, gmla_ref[...]) * gm_ref[0].astype(f32)).astype(bf16)

    @pl.when(qi == 0)
    def _():
        for hh in range(MLA_HEADS):
            qk(hh, 0, even, 0)
        step(0, even, odd, 1)
        finish(1, odd)

    @pl.when(qi > 0)
    def _():
        for hh in range(MLA_HEADS):
            qk(hh, 0, even, None)

    def quad(jj, carry):
        for t in range(0, 4, 2):
            step(4 * jj + t, even, odd, None)
            step(4 * jj + t + 1, odd, even, None)
        return carry

    lax.fori_loop(0, lax.shift_right_logical(jnp.maximum(qi - 1, 0), 1), quad, 0)

    def close():
        step(first_own - 2, even, odd, None)
        step(first_own - 1, odd, even, 0)
        step(first_own, even, odd, 1)
        finish(first_own + 1, odd)

    @pl.when((qi > 0) & (lax.rem(qi, 2) == 1))
    def _():
        close()

    @pl.when((qi > 0) & (lax.rem(qi, 2) == 0))
    def _():
        step(first_own - 4, even, odd, None)
        step(first_own - 3, odd, even, None)
        close()


def _out_kernel(ym_ref, yr_ref, x_ref, wout_ref, gfin_ref, o_ref):
    y = jnp.concatenate([ym_ref[0], yr_ref[0]], axis=-1)
    xo = x_ref[0] + jnp.dot(y, wout_ref[...], preferred_element_type=f32)
    o_ref[0] = _rms(xo, gfin_ref[...])


def _rot_half(w):
    half = QK_ROPE_DIM // 2
    return jnp.concatenate([-w[..., half:], w[..., :half]], axis=-1)


def _prepare_weights(w_in, w_q_b, w_kv_b, w_rg_a, w_rg_x):
    offs = [0, 256, 384, 416, 928, 1440, 1952]
    w_qlat, w_kvlat, w_krope, w_gm, w_xr, w_gr = (
        w_in[:, offs[i]:offs[i + 1]] for i in range(6))
    kpe_blk = jnp.concatenate(
        [jnp.zeros((D_MODEL, QK_NOPE_DIM), w_in.dtype), w_krope, _rot_half(w_krope)], axis=1)
    w_in_r = jnp.concatenate([w_qlat, w_kvlat, kpe_blk, w_gm, w_xr, w_gr], axis=1)

    wq = w_q_b.reshape(Q_LORA_RANK, MLA_HEADS, QK_HEAD_DIM)
    wq_nope = wq[:, :, :QK_NOPE_DIM].reshape(Q_LORA_RANK, -1)
    wq_pe = wq[:, :, QK_NOPE_DIM:]
    wq_r = jnp.concatenate([wq_nope, wq_pe.reshape(Q_LORA_RANK, -1),
                            _rot_half(wq_pe).reshape(Q_LORA_RANK, -1)], axis=1)

    wkv = w_kv_b.reshape(KV_LORA_RANK, MLA_HEADS, QK_NOPE_DIM + V_HEAD_DIM)
    wk = jnp.pad(wkv[:, :, :QK_NOPE_DIM], ((0, 0), (0, 0), (0, HEAD_PAD - QK_NOPE_DIM)))
    wk = wk.reshape(KV_LORA_RANK, MLA_HEADS * HEAD_PAD)
    wv = wkv[:, :, QK_NOPE_DIM:].reshape(KV_LORA_RANK, MLA_WIDTH)
    wkv_r = jnp.concatenate([wk, wv], axis=1)

    per = GATE_GROUP // RNN_BLOCK_DIM
    eye = jnp.eye(per, dtype=w_rg_a.dtype)

    def block_diag(w):
        w = w.reshape(RNN_BLOCKS // per, per, RNN_BLOCK_DIM, RNN_BLOCK_DIM)
        return jnp.einsum('gncd,nm->gncmd', w, eye).reshape(-1, GATE_GROUP, GATE_GROUP)

    wgate = jnp.concatenate([block_diag(w_rg_a), block_diag(w_rg_x)], axis=2)
    return (w_in_r.astype(bf16), wq_r.astype(bf16), wkv_r.astype(bf16), wgate.astype(bf16))


@functools.lru_cache(maxsize=None)
def _rope_tables(seq):
    pos = np.arange(seq, dtype=np.float64)
    inv_freq = ROPE_THETA ** (-np.arange(0, QK_ROPE_DIM, 2, dtype=np.float64) / QK_ROPE_DIM)
    ang = pos[:, None] * inv_freq[None, :]
    cos = np.tile(np.cos(ang), (1, 2))
    sin = np.tile(np.sin(ang), (1, 2))
    scale = math.log2(math.e) / math.sqrt(QK_HEAD_DIM)
    cq = np.tile(cos, (1, MLA_HEADS)) * scale
    sq = np.tile(sin, (1, MLA_HEADS)) * scale
    pads = ((0, 0), (QK_NOPE_DIM, HEAD_PAD - QK_HEAD_DIM))
    return tuple(t.astype(np.float32) for t in (cq, sq, np.pad(cos, pads), np.pad(sin, pads)))


def _const_spec(shape):
    return pl.BlockSpec(shape, lambda *_: (0,) * len(shape))


def kernel(x, norm_in_g, w_in, q_norm_g, w_q_b, kv_norm_g, w_kv_b, conv_w, conv_b,
           w_rg_a, b_rg_a, w_rg_x, b_rg_x, lru_param, out_norm_mla_g, out_norm_rnn_g,
           w_out, final_norm_g):
    b, s, _ = x.shape
    ts = SEQ_TILE
    tq, tk = Q_BLOCK, K_BLOCK
    assert s % ts == 0 and ts % tq == 0 and tq % tk == 0 and tk % CHUNK == 0
    assert s % OUT_TILE == 0
    assert w_in.shape[0] == 1, "the final norm is fused into the single layer's output kernel"
    cq, sq, ck, sk = _rope_tables(s)
    row = lambda v: v.reshape(1, -1).astype(f32)
    params = pltpu.CompilerParams(
        dimension_semantics=("arbitrary", "arbitrary"), vmem_limit_bytes=VMEM_LIMIT)

    w_in_r, wq, wkv, wgate = _prepare_weights(w_in[0], w_q_b[0], w_kv_b[0], w_rg_a[0], w_rg_x[0])
    bgate = jnp.stack([b_rg_a[0], b_rg_x[0]]).astype(f32)

    seq_spec = lambda w: pl.BlockSpec((1, ts, w), lambda bi, si: (bi, si, 0))
    tab_spec = lambda w: pl.BlockSpec((ts, w), lambda bi, si: (si, 0))
    qt, k, vt, gm, yr = pl.pallas_call(
        _proj_kernel,
        grid=(b, s // ts),
        in_specs=[
            seq_spec(D_MODEL), _const_spec((1, D_MODEL)), _const_spec((D_MODEL, Z_WIDTH)),
            _const_spec((1, Q_LORA_RANK)), _const_spec(wq.shape),
            _const_spec((1, KV_LORA_RANK)), _const_spec(wkv.shape),
            tab_spec(Q_PE_WIDTH), tab_spec(Q_PE_WIDTH), tab_spec(HEAD_PAD), tab_spec(HEAD_PAD),
            _const_spec((CONV_WIDTH, RNN_WIDTH)), _const_spec((1, RNN_WIDTH)),
            _const_spec(wgate.shape), _const_spec((2, RNN_WIDTH)),
            _const_spec((1, RNN_WIDTH)), _const_spec((1, RNN_WIDTH)),
        ],
        out_specs=[
            pl.BlockSpec((1, ts // tq, MLA_HEADS, HEAD_PAD, tq), lambda bi, si: (bi, si, 0, 0, 0)),
            pl.BlockSpec((1, MLA_HEADS, ts, HEAD_PAD), lambda bi, si: (bi, 0, si, 0)),
            pl.BlockSpec((1, ts // tk, MLA_HEADS, V_EXT, tk), lambda bi, si: (bi, si, 0, 0, 0)),
            seq_spec(MLA_WIDTH), seq_spec(RNN_WIDTH),
        ],
        out_shape=[
            jax.ShapeDtypeStruct((b, s // tq, MLA_HEADS, HEAD_PAD, tq), bf16),
            jax.ShapeDtypeStruct((b, MLA_HEADS, s, HEAD_PAD), bf16),
            jax.ShapeDtypeStruct((b, s // tk, MLA_HEADS, V_EXT, tk), bf16),
            jax.ShapeDtypeStruct((b, s, MLA_WIDTH), bf16),
            jax.ShapeDtypeStruct((b, s, RNN_WIDTH), bf16),
        ],
        scratch_shapes=[pltpu.VMEM((CONV_WIDTH - 1, SUBLANES, RNN_WIDTH), f32),
                        pltpu.VMEM((1, RNN_WIDTH), f32),
                        pltpu.VMEM((RNN_WIDTH // LANES, SUBLANES * PERM_PITCH, LANES), f32)],
        compiler_params=params,
        name="proj_rnn",
    )(x, row(norm_in_g[0]), w_in_r, row(q_norm_g[0]), wq, row(kv_norm_g[0]), wkv,
      cq, sq, ck, sk, conv_w[0].astype(f32), row(conv_b[0]), wgate, bgate,
      row(lru_param[0]), row(out_norm_rnn_g[0]))

    ym = pl.pallas_call(
        _attn_kernel,
        grid=(b, s // tq),
        in_specs=[
            pl.BlockSpec((1, 1, MLA_HEADS, HEAD_PAD, tq), lambda bi, qi: (bi, qi, 0, 0, 0)),
            pl.BlockSpec((1, MLA_HEADS, s, HEAD_PAD), lambda bi, qi: (bi, 0, 0, 0)),
            pl.BlockSpec((1, s // tk, MLA_HEADS, V_EXT, tk), lambda bi, qi: (bi, 0, 0, 0, 0)),
            pl.BlockSpec((1, tq, MLA_WIDTH), lambda bi, qi: (bi, qi, 0)),
            _const_spec((1, MLA_WIDTH)),
        ],
        out_specs=pl.BlockSpec((1, tq, MLA_WIDTH), lambda bi, qi: (bi, qi, 0)),
        out_shape=jax.ShapeDtypeStruct((b, s, MLA_WIDTH), bf16),
        scratch_shapes=[pltpu.VMEM((MLA_HEADS, tk, tq), f32),
                        pltpu.VMEM((MLA_HEADS, tk, tq), f32),
                        pltpu.VMEM((MLA_HEADS, 1, tq), f32),
                        pltpu.VMEM((MLA_HEADS, 1, tq), f32),
                        pltpu.VMEM((MLA_HEADS, 1, tq), f32),
                        pltpu.VMEM((MLA_HEADS, V_EXT, tq), f32)],
        compiler_params=params,
        name="mla_attention",
    )(qt, k, vt, gm, row(out_norm_mla_g[0]))

    to = OUT_TILE
    out_spec = lambda w: pl.BlockSpec((1, to, w), lambda bi, si: (bi, si, 0))
    return pl.pallas_call(
        _out_kernel,
        grid=(b, s // to),
        in_specs=[
            out_spec(MLA_WIDTH), out_spec(RNN_WIDTH), out_spec(D_MODEL),
            _const_spec((D_MODEL, D_MODEL)), _const_spec((1, D_MODEL)),
        ],
        out_specs=out_spec(D_MODEL),
        out_shape=jax.ShapeDtypeStruct((b, s, D_MODEL), f32),
        compiler_params=params,
        name="out_proj",
    )(ym, yr, x, w_out[0].astype(bf16), row(final_norm_g))
```

```python
import functools
import math

import jax
import jax.numpy as jnp
import numpy as np
from jax import lax
from jax.experimental import pallas as pl
from jax.experimental.pallas import tpu as pltpu

D_MODEL = 1024
CHUNK = 64
MLA_HEADS = 8
QK_NOPE_DIM = 64
QK_ROPE_DIM = 32
QK_HEAD_DIM = QK_NOPE_DIM + QK_ROPE_DIM
V_HEAD_DIM = 64
MLA_WIDTH = MLA_HEADS * V_HEAD_DIM
Q_LORA_RANK = 256
KV_LORA_RANK = 128
ROPE_THETA = 10000.0
RNN_WIDTH = 512
RNN_BLOCKS = 8
RNN_BLOCK_DIM = RNN_WIDTH // RNN_BLOCKS
CONV_WIDTH = 4
LRU_C = 8.0
NORM_EPS = 1e-6

LANES = 128
SUBLANES = 8
MXU_DIM = 256
HEAD_PAD = LANES
V_EXT = V_HEAD_DIM + 16
Z_WIDTH = 2048
SEQ_TILE = 512
OUT_TILE = 1024
K_BLOCK = MXU_DIM
Q_BLOCK = 2 * MXU_DIM
VMEM_LIMIT = 56 * 1024 * 1024

Z_Q, Z_KV, Z_KPE, Z_GM, Z_XR, Z_GR = 0, 256, 384, 512, 1024, 1536
Q_NOPE, Q_PE, Q_ROT = 0, MLA_HEADS * QK_NOPE_DIM, MLA_HEADS * (QK_NOPE_DIM + QK_ROPE_DIM)
Q_PE_WIDTH = MLA_HEADS * QK_ROPE_DIM
GATE_GROUP = MXU_DIM
PERM_PITCH = SEQ_TILE // SUBLANES + SUBLANES

MASK_VALUE = -0.5 * float(jnp.finfo(jnp.float32).max)

f32 = jnp.float32
bf16 = jnp.bfloat16


def _rms(x, g):
    ms = jnp.mean(x * x, axis=-1, keepdims=True)
    return x * lax.rsqrt(ms + NORM_EPS) * g


def _sigmoid(x):
    return 0.5 * jnp.tanh(0.5 * x) + 0.5


def _silu(x):
    hx = 0.5 * x
    return hx * jnp.tanh(hx) + hx


def _to_lockstep(x, buf):
    rows = x.shape[0]
    seg = rows // SUBLANES
    slabs = x.shape[1] // LANES
    for c in range(slabs):
        for r in range(SUBLANES):
            buf[c, r * PERM_PITCH:r * PERM_PITCH + seg, :] = (
                x[r * seg:(r + 1) * seg, c * LANES:(c + 1) * LANES])
    return [jnp.concatenate([buf[c, pl.ds(i, SUBLANES, stride=PERM_PITCH), :]
                             for c in range(slabs)], axis=1) for i in range(seg)]


def _from_lockstep(pieces, buf):
    seg = len(pieces)
    slabs = pieces[0].shape[1] // LANES
    for i, piece in enumerate(pieces):
        for c in range(slabs):
            buf[c, pl.ds(i, SUBLANES, stride=PERM_PITCH), :] = piece[:, c * LANES:(c + 1) * LANES]
    return jnp.concatenate(
        [jnp.concatenate([buf[c, r * PERM_PITCH:r * PERM_PITCH + seg, :] for c in range(slabs)],
                         axis=1) for r in range(SUBLANES)], axis=0)


def _proj_kernel(x_ref, gin_ref, win_ref, qg_ref, wq_ref, kvg_ref, wkv_ref,
                 cq_ref, sq_ref, ck_ref, sk_ref, convw_ref, convb_ref, wgate_ref,
                 bgate_ref, lru_ref, grnn_ref,
                 qt_out, k_out, vt_out, gm_out, yr_out,
                 xr_tail, h_carry, perm_buf):
    s = pl.program_id(1)
    ts = x_ref.shape[1]

    @pl.when(s == 0)
    def _():
        xr_tail[...] = jnp.zeros(xr_tail.shape, f32)
        h_carry[...] = jnp.zeros(h_carry.shape, f32)

    h = _rms(x_ref[0], gin_ref[...]).astype(bf16)

    def in_proj(lo, width):
        return jnp.dot(h, win_ref[:, lo:lo + width], preferred_element_type=f32)

    z_xr = in_proj(Z_XR, RNN_WIDTH)
    z_a = in_proj(0, Z_GM)
    xb, xr = _conv_lockstep(z_xr, xr_tail, perm_buf, convw_ref, convb_ref)
    gates = [jnp.dot(xb[:, g * GATE_GROUP:(g + 1) * GATE_GROUP], wgate_ref[g],
                     preferred_element_type=f32)
             for g in range(RNN_WIDTH // GATE_GROUP)]
    z_gm = in_proj(Z_GM, MLA_WIDTH)
    qn = _rms(z_a[:, Z_Q:Z_Q + Q_LORA_RANK], qg_ref[...]).astype(bf16)
    q = jnp.dot(qn, wq_ref[...], preferred_element_type=f32)
    kvn = _rms(z_a[:, Z_KV:Z_KV + KV_LORA_RANK], kvg_ref[...]).astype(bf16)
    kv = jnp.dot(kvn, wkv_ref[...], preferred_element_type=f32)
    z_gr = in_proj(Z_GR, RNN_WIDTH)

    y = _rg_lru(s, gates, xr, bgate_ref, lru_ref, h_carry, perm_buf)
    yr_out[0] = (_rms(y, grnn_ref[...]) * _silu(z_gr)).astype(bf16)

    scale = math.log2(math.e) / math.sqrt(QK_HEAD_DIM)
    q_nope_t = (q[:, Q_NOPE:Q_PE] * scale).T
    q_pe = q[:, Q_PE:Q_ROT] * cq_ref[...] + q[:, Q_ROT:] * sq_ref[...]
    q_pe_t = q_pe.T
    pad = jnp.zeros((HEAD_PAD - QK_HEAD_DIM, Q_BLOCK), bf16)
    for j in range(ts // Q_BLOCK):
        cols = slice(j * Q_BLOCK, (j + 1) * Q_BLOCK)
        for hh in range(MLA_HEADS):
            qt_out[0, j, hh, :QK_NOPE_DIM, :] = (
                q_nope_t[hh * QK_NOPE_DIM:(hh + 1) * QK_NOPE_DIM, cols].astype(bf16))
            qt_out[0, j, hh, QK_NOPE_DIM:QK_HEAD_DIM, :] = (
                q_pe_t[hh * QK_ROPE_DIM:(hh + 1) * QK_ROPE_DIM, cols].astype(bf16))
            qt_out[0, j, hh, QK_HEAD_DIM:, :] = pad

    kblk = z_a[:, Z_KPE:Z_KPE + HEAD_PAD]
    kpe = kblk * ck_ref[...] + pltpu.roll(kblk, LANES - QK_ROPE_DIM, 1) * sk_ref[...]
    for hh in range(MLA_HEADS):
        k_out[0, hh] = (kv[:, hh * HEAD_PAD:(hh + 1) * HEAD_PAD] + kpe).astype(bf16)
    vt = kv[:, MLA_HEADS * HEAD_PAD:].T
    ones_row = lax.broadcasted_iota(jnp.int32, (MLA_HEADS, V_EXT - V_HEAD_DIM, K_BLOCK), 1) == 0
    for j in range(ts // K_BLOCK):
        blk = vt[:, j * K_BLOCK:(j + 1) * K_BLOCK]
        vt_out[0, j, :, :V_HEAD_DIM, :] = (
            blk.reshape(MLA_HEADS, V_HEAD_DIM, K_BLOCK).astype(bf16))
        vt_out[0, j, :, V_HEAD_DIM:, :] = jnp.where(ones_row, 1.0, 0.0).astype(bf16)

    gm_out[0] = _silu(z_gm).astype(bf16)


def _conv_lockstep(x_rnn, xr_tail, perm_buf, convw_ref, convb_ref):
    ts = x_rnn.shape[0]
    sub8 = lax.broadcasted_iota(jnp.int32, (SUBLANES, RNN_WIDTH), 0)
    xl = _to_lockstep(x_rnn, perm_buf)
    hist = []
    for k in range(CONV_WIDTH - 1):
        prev = xr_tail[k]
        cur = xl[len(xl) - (CONV_WIDTH - 1) + k]
        hist.append(pltpu.roll(jnp.where(sub8 == SUBLANES - 1, prev, cur), 1, 0))
        xr_tail[k] = cur
    xext = jnp.concatenate(hist + xl, axis=0)
    xr = convb_ref[...]
    for k in range(CONV_WIDTH):
        xr = xr + convw_ref[k:k + 1, :] * xext[k * SUBLANES:k * SUBLANES + ts]
    return xr.astype(bf16), xr


def _rg_lru(s, gates, xr, bgate_ref, lru_ref, h_carry, perm_buf):
    ts = xr.shape[0]
    sub8 = lax.broadcasted_iota(jnp.int32, (SUBLANES, RNN_WIDTH), 0)
    ga = [gg[:, :GATE_GROUP] for gg in gates]
    gx = [gg[:, GATE_GROUP:] for gg in gates]
    r = _sigmoid(jnp.concatenate(ga, axis=1) + bgate_ref[0:1, :])
    i = _sigmoid(jnp.concatenate(gx, axis=1) + bgate_ref[1:2, :])
    nl = -lru_ref[...]
    softplus = jnp.maximum(nl, 0.0) + jnp.log1p(jnp.exp(-jnp.abs(nl)))
    a = jnp.exp2(r * ((-LRU_C * math.log2(math.e)) * softplus))
    m2 = jnp.maximum(1.0 - a * a, 1e-12)
    mult = m2 * lax.rsqrt(m2)
    row = lax.broadcasted_iota(jnp.int32, (ts, RNN_WIDTH), 0)
    mult = jnp.where((row == 0) & (s == 0), 1.0, mult)
    u = mult * (i * xr)

    steps = ts // SUBLANES
    hl, pr = [], []
    hcur = pcur = None
    for t in range(steps):
        at = a[t * SUBLANES:(t + 1) * SUBLANES]
        ut = u[t * SUBLANES:(t + 1) * SUBLANES]
        hcur = ut if t == 0 else at * hcur + ut
        pcur = at if t == 0 else at * pcur
        hl.append(hcur)
        pr.append(pcur)
    pe, he = pcur, hcur
    for d in (1, 2, 4):
        keep = sub8 >= d
        he = jnp.where(keep, pe * pltpu.roll(he, d, 0) + he, he)
        pe = jnp.where(keep, pe * pltpu.roll(pe, d, 0), pe)
    carry = h_carry[...]
    ends = he + pe * carry
    h_carry[...] = ends[SUBLANES - 1:SUBLANES, :]
    init = jnp.where(sub8 == 0, carry, pltpu.roll(ends, 1, 0))
    return _from_lockstep([hl[t] + pr[t] * init for t in range(steps)], perm_buf)


def _attn_kernel(qt_ref, k_ref, vt_ref, gm_ref, gmla_ref, o_ref,
                 s_even, s_odd, c_even, c_odd, m_ref, acc_ref):
    qi = pl.program_id(1)
    tk = K_BLOCK
    per_q = Q_BLOCK // K_BLOCK
    first_own = per_q * qi
    even = (s_even, c_even)
    odd = (s_odd, c_odd)

    m_ref[...] = jnp.full(m_ref.shape, MASK_VALUE, f32)
    acc_ref[...] = jnp.zeros(acc_ref.shape, f32)

    def qk(hh, j, dst, own):
        s_dst, c_dst = dst
        kb = k_ref[0, hh, pl.ds(pl.multiple_of(j * tk, tk), tk), :]
        st = jnp.dot(kb, qt_ref[0, 0, hh], preferred_element_type=f32)
        if own is not None:
            kc = (lax.broadcasted_iota(jnp.int32, st.shape, 0) + own * tk) // CHUNK
            qc = lax.broadcasted_iota(jnp.int32, st.shape, 1) // CHUNK
            st = jnp.where(kc <= qc, st, MASK_VALUE)
        s_dst[hh] = st
        c_dst[hh] = jnp.max(st, axis=0, keepdims=True)

    def attend(hh, j, src):
        s_src, c_src = src
        m_prev = m_ref[hh]
        m_new = jnp.maximum(m_prev, c_src[hh])
        alpha = jnp.exp2(m_prev - m_new)
        p = jnp.exp2(s_src[hh] - m_new).astype(bf16)
        upd = jnp.dot(vt_ref[0, j, hh], p, preferred_element_type=f32)
        acc_ref[hh] = alpha * acc_ref[hh] + upd
        m_ref[hh] = m_new

    def step(j, cur, nxt, next_own):
        for hh in range(MLA_HEADS):
            qk(hh, j + 1, nxt, next_own)
            attend(hh, j, cur)

    def finish(j, cur):
        heads = []
        for hh in range(MLA_HEADS):
            attend(hh, j, cur)
            acc = acc_ref[hh]
            heads.append(acc[:V_HEAD_DIM] / acc[V_HEAD_DIM:V_HEAD_DIM + 1])
        y = jnp.concatenate(heads, axis=0).T
        o_ref[0] = (_rms(y, gmla_ref[...]) * gm_ref[0].astype(f32)).astype(bf16)

    @pl.when(qi == 0)
    def _():
        for hh in range(MLA_HEADS):
            qk(hh, 0, even, 0)
        step(0, even, odd, 1)
        finish(1, odd)

    @pl.when(qi > 0)
    def _():
        for hh in range(MLA_HEADS):
            qk(hh, 0, even, None)

    def quad(jj, carry):
        for t in range(0, 4, 2):
            step(4 * jj + t, even, odd, None)
            step(4 * jj + t + 1, odd, even, None)
        return carry

    lax.fori_loop(0, lax.shift_right_logical(jnp.maximum(qi - 1, 0), 1), quad, 0)

    def close():
        step(first_own - 2, even, odd, None)
        step(first_own - 1, odd, even, 0)
        step(first_own, even, odd, 1)
        finish(first_own + 1, odd)

    @pl.when((qi > 0) & (lax.rem(qi, 2) == 1))
    def _():
        close()

    @pl.when((qi > 0) & (lax.rem(qi, 2) == 0))
    def _():
        step(first_own - 4, even, odd, None)
        step(first_own - 3, odd, even, None)
        close()


def _out_kernel(ym_ref, yr_ref, x_ref, wout_ref, gfin_ref, o_ref):
    y = jnp.concatenate([ym_ref[0], yr_ref[0]], axis=-1)
    xo = x_ref[0] + jnp.dot(y, wout_ref[...], preferred_element_type=f32)
    o_ref[0] = _rms(xo, gfin_ref[...])


def _rot_half(w):
    half = QK_ROPE_DIM // 2
    return jnp.concatenate([-w[..., half:], w[..., :half]], axis=-1)


def _win_layout_kernel(w_ref, o_ref):
    w = w_ref[...]
    lat = Q_LORA_RANK + KV_LORA_RANK
    rest = lat + QK_ROPE_DIM
    kr = w[:, lat:rest]
    blk = jnp.concatenate([jnp.zeros((w.shape[0], QK_NOPE_DIM), f32), kr, _rot_half(kr)], axis=1)
    o_ref[:, :lat] = w[:, :lat].astype(bf16)
    o_ref[:, lat:Z_GM] = blk.astype(bf16)
    o_ref[:, Z_GM:] = w[:, rest:].astype(bf16)


def _prepare_weights(w_in, w_q_b, w_kv_b, w_rg_a, w_rg_x):
    rows = 256
    w_in_r = pl.pallas_call(
        _win_layout_kernel,
        grid=(D_MODEL // rows,),
        in_specs=[pl.BlockSpec((rows, w_in.shape[1]), lambda i: (i, 0))],
        out_specs=pl.BlockSpec((rows, Z_WIDTH), lambda i: (i, 0)),
        out_shape=jax.ShapeDtypeStruct((D_MODEL, Z_WIDTH), bf16),
        name="w_in_layout",
    )(w_in)

    wq = w_q_b.reshape(Q_LORA_RANK, MLA_HEADS, QK_HEAD_DIM)
    wq_nope = wq[:, :, :QK_NOPE_DIM].reshape(Q_LORA_RANK, -1)
    wq_pe = wq[:, :, QK_NOPE_DIM:]
    wq_r = jnp.concatenate([wq_nope, wq_pe.reshape(Q_LORA_RANK, -1),
                            _rot_half(wq_pe).reshape(Q_LORA_RANK, -1)], axis=1)

    wkv = w_kv_b.reshape(KV_LORA_RANK, MLA_HEADS, QK_NOPE_DIM + V_HEAD_DIM)
    wk = jnp.pad(wkv[:, :, :QK_NOPE_DIM], ((0, 0), (0, 0), (0, HEAD_PAD - QK_NOPE_DIM)))
    wk = wk.reshape(KV_LORA_RANK, MLA_HEADS * HEAD_PAD)
    wv = wkv[:, :, QK_NOPE_DIM:].reshape(KV_LORA_RANK, MLA_WIDTH)
    wkv_r = jnp.concatenate([wk, wv], axis=1)

    per = GATE_GROUP // RNN_BLOCK_DIM
    eye = jnp.eye(per, dtype=w_rg_a.dtype)

    def block_diag(w):
        w = w.reshape(RNN_BLOCKS // per, per, RNN_BLOCK_DIM, RNN_BLOCK_DIM)
        return jnp.einsum('gncd,nm->gncmd', w, eye).reshape(-1, GATE_GROUP, GATE_GROUP)

    wgate = jnp.concatenate([block_diag(w_rg_a), block_diag(w_rg_x)], axis=2)
    return (w_in_r, wq_r.astype(bf16), wkv_r.astype(bf16), wgate.astype(bf16))


@functools.lru_cache(maxsize=None)
def _rope_tables(seq):
    pos = np.arange(seq, dtype=np.float64)
    inv_freq = ROPE_THETA ** (-np.arange(0, QK_ROPE_DIM, 2, dtype=np.float64) / QK_ROPE_DIM)
    ang = pos[:, None] * inv_freq[None, :]
    cos = np.tile(np.cos(ang), (1, 2))
    sin = np.tile(np.sin(ang), (1, 2))
    scale = math.log2(math.e) / math.sqrt(QK_HEAD_DIM)
    cq = np.tile(cos, (1, MLA_HEADS)) * scale
    sq = np.tile(sin, (1, MLA_HEADS)) * scale
    pads = ((0, 0), (QK_NOPE_DIM, HEAD_PAD - QK_HEAD_DIM))
    return tuple(t.astype(np.float32) for t in (cq, sq, np.pad(cos, pads), np.pad(sin, pads)))


def _const_spec(shape):
    return pl.BlockSpec(shape, lambda *_: (0,) * len(shape))


def kernel(x, norm_in_g, w_in, q_norm_g, w_q_b, kv_norm_g, w_kv_b, conv_w, conv_b,
           w_rg_a, b_rg_a, w_rg_x, b_rg_x, lru_param, out_norm_mla_g, out_norm_rnn_g,
           w_out, final_norm_g):
    b, s, _ = x.shape
    ts = SEQ_TILE
    tq, tk = Q_BLOCK, K_BLOCK
    assert s % ts == 0 and ts % tq == 0 and tq % tk == 0 and tk % CHUNK == 0
    assert s % OUT_TILE == 0
    assert w_in.shape[0] == 1, "the final norm is fused into the single layer's output kernel"
    cq, sq, ck, sk = _rope_tables(s)
    row = lambda v: v.reshape(1, -1).astype(f32)
    params = pltpu.CompilerParams(
        dimension_semantics=("arbitrary", "arbitrary"), vmem_limit_bytes=VMEM_LIMIT)

    w_in_r, wq, wkv, wgate = _prepare_weights(w_in[0], w_q_b[0], w_kv_b[0], w_rg_a[0], w_rg_x[0])
    bgate = jnp.stack([b_rg_a[0], b_rg_x[0]]).astype(f32)

    seq_spec = lambda w: pl.BlockSpec((1, ts, w), lambda bi, si: (bi, si, 0))
    tab_spec = lambda w: pl.BlockSpec((ts, w), lambda bi, si: (si, 0))
    qt, k, vt, gm, yr = pl.pallas_call(
        _proj_kernel,
        grid=(b, s // ts),
        in_specs=[
            seq_spec(D_MODEL), _const_spec((1, D_MODEL)), _const_spec((D_MODEL, Z_WIDTH)),
            _const_spec((1, Q_LORA_RANK)), _const_spec(wq.shape),
            _const_spec((1, KV_LORA_RANK)), _const_spec(wkv.shape),
            tab_spec(Q_PE_WIDTH), tab_spec(Q_PE_WIDTH), tab_spec(HEAD_PAD), tab_spec(HEAD_PAD),
            _const_spec((CONV_WIDTH, RNN_WIDTH)), _const_spec((1, RNN_WIDTH)),
            _const_spec(wgate.shape), _const_spec((2, RNN_WIDTH)),
            _const_spec((1, RNN_WIDTH)), _const_spec((1, RNN_WIDTH)),
        ],
        out_specs=[
            pl.BlockSpec((1, ts // tq, MLA_HEADS, HEAD_PAD, tq), lambda bi, si: (bi, si, 0, 0, 0)),
            pl.BlockSpec((1, MLA_HEADS, ts, HEAD_PAD), lambda bi, si: (bi, 0, si, 0)),
            pl.BlockSpec((1, ts // tk, MLA_HEADS, V_EXT, tk), lambda bi, si: (bi, si, 0, 0, 0)),
            seq_spec(MLA_WIDTH), seq_spec(RNN_WIDTH),
        ],
        out_shape=[
            jax.ShapeDtypeStruct((b, s // tq, MLA_HEADS, HEAD_PAD, tq), bf16),
            jax.ShapeDtypeStruct((b, MLA_HEADS, s, HEAD_PAD), bf16),
            jax.ShapeDtypeStruct((b, s // tk, MLA_HEADS, V_EXT, tk), bf16),
            jax.ShapeDtypeStruct((b, s, MLA_WIDTH), bf16),
            jax.ShapeDtypeStruct((b, s, RNN_WIDTH), bf16),
        ],
        scratch_shapes=[pltpu.VMEM((CONV_WIDTH - 1, SUBLANES, RNN_WIDTH), f32),
                        pltpu.VMEM((1, RNN_WIDTH), f32),
                        pltpu.VMEM((RNN_WIDTH // LANES, SUBLANES * PERM_PITCH, LANES), f32)],
        compiler_params=params,
        name="proj_rnn",
    )(x, row(norm_in_g[0]), w_in_r, row(q_norm_g[0]), wq, row(kv_norm_g[0]), wkv,
      cq, sq, ck, sk, conv_w[0].astype(f32), row(conv_b[0]), wgate, bgate,
      row(lru_param[0]), row(out_norm_rnn_g[0]))

    ym = pl.pallas_call(
        _attn_kernel,
        grid=(b, s // tq),
        in_specs=[
            pl.BlockSpec((1, 1, MLA_HEADS, HEAD_PAD, tq), lambda bi, qi: (bi, qi, 0, 0, 0)),
            pl.BlockSpec((1, MLA_HEADS, s, HEAD_PAD), lambda bi, qi: (bi, 0, 0, 0)),
            pl.BlockSpec((1, s // tk, MLA_HEADS, V_EXT, tk), lambda bi, qi: (bi, 0, 0, 0, 0)),
            pl.BlockSpec((1, tq, MLA_WIDTH), lambda bi, qi: (bi, qi, 0)),
            _const_spec((1, MLA_WIDTH)),
        ],
        out_specs=pl.BlockSpec((1, tq, MLA_WIDTH), lambda bi, qi: (bi, qi, 0)),
        out_shape=jax.ShapeDtypeStruct((b, s, MLA_WIDTH), bf16),
        scratch_shapes=[pltpu.VMEM((MLA_HEADS, tk, tq), f32),
                        pltpu.VMEM((MLA_HEADS, tk, tq), f32),
                        pltpu.VMEM((MLA_HEADS, 1, tq), f32),
                        pltpu.VMEM((MLA_HEADS, 1, tq), f32),
                        pltpu.VMEM((MLA_HEADS, 1, tq), f32),
                        pltpu.VMEM((MLA_HEADS, V_EXT, tq), f32)],
        compiler_params=params,
        name="mla_attention",
    )(qt, k, vt, gm, row(out_norm_mla_g[0]))

    to = OUT_TILE
    out_spec = lambda w: pl.BlockSpec((1, to, w), lambda bi, si: (bi, si, 0))
    return pl.pallas_call(
        _out_kernel,
        grid=(b, s // to),
        in_specs=[
            out_spec(MLA_WIDTH), out_spec(RNN_WIDTH), out_spec(D_MODEL),
            _const_spec((D_MODEL, D_MODEL)), _const_spec((1, D_MODEL)),
        ],
        out_specs=out_spec(D_MODEL),
        out_shape=jax.ShapeDtypeStruct((b, s, D_MODEL), f32),
        compiler_params=params,
        name="out_proj",
    )(ym, yr, x, w_out[0].astype(bf16), row(final_norm_g))
```

```python
import functools
import math

import jax
import jax.numpy as jnp
import numpy as np
from jax import lax
from jax.experimental import pallas as pl
from jax.experimental.pallas import tpu as pltpu

D_MODEL = 1024
CHUNK = 64
MLA_HEADS = 8
QK_NOPE_DIM = 64
QK_ROPE_DIM = 32
QK_HEAD_DIM = QK_NOPE_DIM + QK_ROPE_DIM
V_HEAD_DIM = 64
MLA_WIDTH = MLA_HEADS * V_HEAD_DIM
Q_LORA_RANK = 256
KV_LORA_RANK = 128
ROPE_THETA = 10000.0
RNN_WIDTH = 512
RNN_BLOCKS = 8
RNN_BLOCK_DIM = RNN_WIDTH // RNN_BLOCKS
CONV_WIDTH = 4
LRU_C = 8.0
NORM_EPS = 1e-6

LANES = 128
SUBLANES = 8
MXU_DIM = 256
HEAD_PAD = LANES
V_EXT = V_HEAD_DIM + 16
Z_WIDTH = 2048
SEQ_TILE = 512
OUT_TILE = 1024
K_BLOCK = MXU_DIM
Q_BLOCK = 2 * MXU_DIM
VMEM_LIMIT = 56 * 1024 * 1024

Z_Q, Z_KV, Z_KPE, Z_GM, Z_XR, Z_GR = 0, 256, 384, 512, 1024, 1536
Q_NOPE, Q_PE, Q_ROT = 0, MLA_HEADS * QK_NOPE_DIM, MLA_HEADS * (QK_NOPE_DIM + QK_ROPE_DIM)
Q_PE_WIDTH = MLA_HEADS * QK_ROPE_DIM
GATE_GROUP = MXU_DIM
PERM_PITCH = SEQ_TILE // SUBLANES + SUBLANES

MASK_VALUE = -0.5 * float(jnp.finfo(jnp.float32).max)

f32 = jnp.float32
bf16 = jnp.bfloat16


def _rms(x, g):
    ms = jnp.mean(x * x, axis=-1, keepdims=True)
    return x * lax.rsqrt(ms + NORM_EPS) * g


def _sigmoid(x):
    return 0.5 * jnp.tanh(0.5 * x) + 0.5


def _silu(x):
    hx = 0.5 * x
    return hx * jnp.tanh(hx) + hx


def _to_lockstep(x, buf):
    rows = x.shape[0]
    seg = rows // SUBLANES
    slabs = x.shape[1] // LANES
    for c in range(slabs):
        for r in range(SUBLANES):
            buf[c, r * PERM_PITCH:r * PERM_PITCH + seg, :] = (
                x[r * seg:(r + 1) * seg, c * LANES:(c + 1) * LANES])
    return [jnp.concatenate([buf[c, pl.ds(i, SUBLANES, stride=PERM_PITCH), :]
                             for c in range(slabs)], axis=1) for i in range(seg)]


def _from_lockstep(pieces, buf):
    seg = len(pieces)
    slabs = pieces[0].shape[1] // LANES
    for i, piece in enumerate(pieces):
        for c in range(slabs):
            buf[c, pl.ds(i, SUBLANES, stride=PERM_PITCH), :] = piece[:, c * LANES:(c + 1) * LANES]
    return jnp.concatenate(
        [jnp.concatenate([buf[c, r * PERM_PITCH:r * PERM_PITCH + seg, :] for c in range(slabs)],
                         axis=1) for r in range(SUBLANES)], axis=0)


def _proj_kernel(x_ref, gin_ref, win_ref, qg_ref, wq_ref, kvg_ref, wkv_ref,
                 cq_ref, sq_ref, ck_ref, sk_ref, convw_ref, convb_ref, wgate_ref,
                 bgate_ref, lru_ref, grnn_ref,
                 qt_out, k_out, vt_out, gm_out, yr_out,
                 xr_tail, h_carry, perm_buf):
    s = pl.program_id(1)
    ts = x_ref.shape[1]

    @pl.when(s == 0)
    def _():
        xr_tail[...] = jnp.zeros(xr_tail.shape, f32)
        h_carry[...] = jnp.zeros(h_carry.shape, f32)

    h = _rms(x_ref[0], gin_ref[...]).astype(bf16)

    def in_proj(lo, width):
        return jnp.dot(h, win_ref[:, lo:lo + width], preferred_element_type=f32)

    z_xr = in_proj(Z_XR, RNN_WIDTH)
    z_a = in_proj(0, Z_GM)
    xb, xr = _conv_lockstep(z_xr, xr_tail, perm_buf, convw_ref, convb_ref)
    gates = [jnp.dot(xb[:, g * GATE_GROUP:(g + 1) * GATE_GROUP], wgate_ref[g],
                     preferred_element_type=f32)
             for g in range(RNN_WIDTH // GATE_GROUP)]
    z_gm = in_proj(Z_GM, MLA_WIDTH)
    qn = _rms(z_a[:, Z_Q:Z_Q + Q_LORA_RANK], qg_ref[...]).astype(bf16)
    q = jnp.dot(qn, wq_ref[...], preferred_element_type=f32)
    kvn = _rms(z_a[:, Z_KV:Z_KV + KV_LORA_RANK], kvg_ref[...]).astype(bf16)
    kv = jnp.dot(kvn, wkv_ref[...], preferred_element_type=f32)
    z_gr = in_proj(Z_GR, RNN_WIDTH)

    y = _rg_lru(s, gates, xr, bgate_ref, lru_ref, h_carry, perm_buf)
    yr_out[0] = (_rms(y, grnn_ref[...]) * _silu(z_gr)).astype(bf16)

    scale = math.log2(math.e) / math.sqrt(QK_HEAD_DIM)
    q_nope_t = (q[:, Q_NOPE:Q_PE] * scale).T
    q_pe = q[:, Q_PE:Q_ROT] * cq_ref[...] + q[:, Q_ROT:] * sq_ref[...]
    q_pe_t = q_pe.T
    pad = jnp.zeros((HEAD_PAD - QK_HEAD_DIM, Q_BLOCK), bf16)
    for j in range(ts // Q_BLOCK):
        cols = slice(j * Q_BLOCK, (j + 1) * Q_BLOCK)
        for hh in range(MLA_HEADS):
            qt_out[0, j, hh, :QK_NOPE_DIM, :] = (
                q_nope_t[hh * QK_NOPE_DIM:(hh + 1) * QK_NOPE_DIM, cols].astype(bf16))
            qt_out[0, j, hh, QK_NOPE_DIM:QK_HEAD_DIM, :] = (
                q_pe_t[hh * QK_ROPE_DIM:(hh + 1) * QK_ROPE_DIM, cols].astype(bf16))
            qt_out[0, j, hh, QK_HEAD_DIM:, :] = pad

    kblk = z_a[:, Z_KPE:Z_KPE + HEAD_PAD]
    kpe = kblk * ck_ref[...] + pltpu.roll(kblk, LANES - QK_ROPE_DIM, 1) * sk_ref[...]
    for hh in range(MLA_HEADS):
        k_out[0, hh] = (kv[:, hh * HEAD_PAD:(hh + 1) * HEAD_PAD] + kpe).astype(bf16)
    vt = kv[:, MLA_HEADS * HEAD_PAD:].T
    ones_row = lax.broadcasted_iota(jnp.int32, (MLA_HEADS, V_EXT - V_HEAD_DIM, K_BLOCK), 1) == 0
    for j in range(ts // K_BLOCK):
        blk = vt[:, j * K_BLOCK:(j + 1) * K_BLOCK]
        vt_out[0, j, :, :V_HEAD_DIM, :] = (
            blk.reshape(MLA_HEADS, V_HEAD_DIM, K_BLOCK).astype(bf16))
        vt_out[0, j, :, V_HEAD_DIM:, :] = jnp.where(ones_row, 1.0, 0.0).astype(bf16)

    gm_out[0] = _silu(z_gm).astype(bf16)


def _conv_lockstep(x_rnn, xr_tail, perm_buf, convw_ref, convb_ref):
    ts = x_rnn.shape[0]
    sub8 = lax.broadcasted_iota(jnp.int32, (SUBLANES, RNN_WIDTH), 0)
    xl = _to_lockstep(x_rnn, perm_buf)
    hist = []
    for k in range(CONV_WIDTH - 1):
        prev = xr_tail[k]
        cur = xl[len(xl) - (CONV_WIDTH - 1) + k]
        hist.append(pltpu.roll(jnp.where(sub8 == SUBLANES - 1, prev, cur), 1, 0))
        xr_tail[k] = cur
    xext = jnp.concatenate(hist + xl, axis=0)
    xr = convb_ref[...]
    for k in range(CONV_WIDTH):
        xr = xr + convw_ref[k:k + 1, :] * xext[k * SUBLANES:k * SUBLANES + ts]
    return xr.astype(bf16), xr


def _rg_lru(s, gates, xr, bgate_ref, lru_ref, h_carry, perm_buf):
    ts = xr.shape[0]
    sub8 = lax.broadcasted_iota(jnp.int32, (SUBLANES, RNN_WIDTH), 0)
    ga = [gg[:, :GATE_GROUP] for gg in gates]
    gx = [gg[:, GATE_GROUP:] for gg in gates]
    r = _sigmoid(jnp.concatenate(ga, axis=1) + bgate_ref[0:1, :])
    i = _sigmoid(jnp.concatenate(gx, axis=1) + bgate_ref[1:2, :])
    nl = -lru_ref[...]
    softplus = jnp.maximum(nl, 0.0) + jnp.log1p(jnp.exp(-jnp.abs(nl)))
    a = jnp.exp2(r * ((-LRU_C * math.log2(math.e)) * softplus))
    m2 = jnp.maximum(1.0 - a * a, 1e-12)
    mult = m2 * lax.rsqrt(m2)
    row = lax.broadcasted_iota(jnp.int32, (ts, RNN_WIDTH), 0)
    mult = jnp.where((row == 0) & (s == 0), 1.0, mult)
    u = mult * (i * xr)

    steps = ts // SUBLANES
    hl, pr = [], []
    hcur = pcur = None
    for t in range(steps):
        at = a[t * SUBLANES:(t + 1) * SUBLANES]
        ut = u[t * SUBLANES:(t + 1) * SUBLANES]
        hcur = ut if t == 0 else at * hcur + ut
        pcur = at if t == 0 else at * pcur
        hl.append(hcur)
        pr.append(pcur)
    pe, he = pcur, hcur
    for d in (1, 2, 4):
        keep = sub8 >= d
        he = jnp.where(keep, pe * pltpu.roll(he, d, 0) + he, he)
        pe = jnp.where(keep, pe * pltpu.roll(pe, d, 0), pe)
    carry = h_carry[...]
    ends = he + pe * carry
    h_carry[...] = ends[SUBLANES - 1:SUBLANES, :]
    init = jnp.where(sub8 == 0, carry, pltpu.roll(ends, 1, 0))
    return _from_lockstep([hl[t] + pr[t] * init for t in range(steps)], perm_buf)


def _attn_kernel(qt_ref, k_ref, vt_ref, gm_ref, gmla_ref, o_ref,
                 s_even, s_odd, c_even, c_odd, m_ref, acc_ref):
    qi = pl.program_id(1)
    tk = K_BLOCK
    per_q = Q_BLOCK // K_BLOCK
    first_own = per_q * qi
    even = (s_even, c_even)
    odd = (s_odd, c_odd)

    m_ref[...] = jnp.full(m_ref.shape, MASK_VALUE, f32)
    acc_ref[...] = jnp.zeros(acc_ref.shape, f32)

    def qk(hh, j, dst, own):
        s_dst, c_dst = dst
        kb = k_ref[0, hh, pl.ds(pl.multiple_of(j * tk, tk), tk), :]
        st = jnp.dot(kb, qt_ref[0, 0, hh], preferred_element_type=f32)
        if own is not None:
            kc = (lax.broadcasted_iota(jnp.int32, st.shape, 0) + own * tk) // CHUNK
            qc = lax.broadcasted_iota(jnp.int32, st.shape, 1) // CHUNK
            st = jnp.where(kc <= qc, st, MASK_VALUE)
        s_dst[hh] = st
        c_dst[hh] = jnp.max(st, axis=0, keepdims=True)

    def attend(hh, j, src):
        s_src, c_src = src
        m_prev = m_ref[hh]
        m_new = jnp.maximum(m_prev, c_src[hh])
        alpha = jnp.exp2(m_prev - m_new)
        p = jnp.exp2(s_src[hh] - m_new).astype(bf16)
        upd = jnp.dot(vt_ref[0, j, hh], p, preferred_element_type=f32)
        acc_ref[hh] = alpha * acc_ref[hh] + upd
        m_ref[hh] = m_new

    def step(j, cur, nxt, next_own):
        for hh in range(MLA_HEADS):
            qk(hh, j + 1, nxt, next_own)
            attend(hh, j, cur)

    def finish(j, cur):
        heads = []
        for hh in range(MLA_HEADS):
            attend(hh, j, cur)
            acc = acc_ref[hh]
            heads.append(acc[:V_HEAD_DIM] / acc[V_HEAD_DIM:V_HEAD_DIM + 1])
        y = jnp.concatenate(heads, axis=0).T
        o_ref[0] = (_rms(y, gmla_ref[...]) * gm_ref[0].astype(f32)).astype(bf16)

    @pl.when(qi == 0)
    def _():
        for hh in range(MLA_HEADS):
            qk(hh, 0, even, 0)
        step(0, even, odd, 1)
        finish(1, odd)

    @pl.when(qi > 0)
    def _():
        for hh in range(MLA_HEADS):
            qk(hh, 0, even, None)

    def quad(jj, carry):
        for t in range(0, 4, 2):
            step(4 * jj + t, even, odd, None)
            step(4 * jj + t + 1, odd, even, None)
        return carry

    lax.fori_loop(0, lax.shift_right_logical(jnp.maximum(qi - 1, 0), 1), quad, 0)

    def close():
        step(first_own - 2, even, odd, None)
        step(first_own - 1, odd, even, 0)
        step(first_own, even, odd, 1)
        finish(first_own + 1, odd)

    @pl.when((qi > 0) & (lax.rem(qi, 2) == 1))
    def _():
        close()

    @pl.when((qi > 0) & (lax.rem(qi, 2) == 0))
    def _():
        step(first_own - 4, even, odd, None)
        step(first_own - 3, odd, even, None)
        close()


def _out_kernel(ym_ref, yr_ref, x_ref, wout_ref, gfin_ref, o_ref):
    y = jnp.concatenate([ym_ref[0], yr_ref[0]], axis=-1)
    xo = x_ref[0] + jnp.dot(y, wout_ref[...], preferred_element_type=f32)
    o_ref[0] = _rms(xo, gfin_ref[...])


def _rot_half(w):
    half = QK_ROPE_DIM // 2
    return jnp.concatenate([-w[..., half:], w[..., :half]], axis=-1)


def _win_layout_kernel(wt_ref, o_ref):
    wt = wt_ref[...]
    lat = Q_LORA_RANK + KV_LORA_RANK
    rest = lat + QK_ROPE_DIM
    half = QK_ROPE_DIM // 2
    kr = wt[lat:rest]
    rows = jnp.concatenate(
        [wt[:lat], jnp.zeros((QK_NOPE_DIM, wt.shape[1]), f32), kr, -kr[half:], kr[:half],
         wt[rest:]], axis=0)
    o_ref[...] = rows.T.astype(bf16)


def _prepare_weights(w_in, w_q_b, w_kv_b, w_rg_a, w_rg_x):
    kb = 256
    w_in_r = pl.pallas_call(
        _win_layout_kernel,
        grid=(D_MODEL // kb,),
        in_specs=[pl.BlockSpec((w_in.shape[1], kb), lambda i: (0, i))],
        out_specs=pl.BlockSpec((kb, Z_WIDTH), lambda i: (i, 0)),
        out_shape=jax.ShapeDtypeStruct((D_MODEL, Z_WIDTH), bf16),
        name="w_in_layout",
    )(w_in.T)

    wq = w_q_b.reshape(Q_LORA_RANK, MLA_HEADS, QK_HEAD_DIM)
    wq_nope = wq[:, :, :QK_NOPE_DIM].reshape(Q_LORA_RANK, -1)
    wq_pe = wq[:, :, QK_NOPE_DIM:]
    wq_r = jnp.concatenate([wq_nope, wq_pe.reshape(Q_LORA_RANK, -1),
                            _rot_half(wq_pe).reshape(Q_LORA_RANK, -1)], axis=1)

    wkv = w_kv_b.reshape(KV_LORA_RANK, MLA_HEADS, QK_NOPE_DIM + V_HEAD_DIM)
    wk = jnp.pad(wkv[:, :, :QK_NOPE_DIM], ((0, 0), (0, 0), (0, HEAD_PAD - QK_NOPE_DIM)))
    wk = wk.reshape(KV_LORA_RANK, MLA_HEADS * HEAD_PAD)
    wv = wkv[:, :, QK_NOPE_DIM:].reshape(KV_LORA_RANK, MLA_WIDTH)
    wkv_r = jnp.concatenate([wk, wv], axis=1)

    per = GATE_GROUP // RNN_BLOCK_DIM
    eye = jnp.eye(per, dtype=w_rg_a.dtype)

    def block_diag(w):
        w = w.reshape(RNN_BLOCKS // per, per, RNN_BLOCK_DIM, RNN_BLOCK_DIM)
        return jnp.einsum('gncd,nm->gncmd', w, eye).reshape(-1, GATE_GROUP, GATE_GROUP)

    wgate = jnp.concatenate([block_diag(w_rg_a), block_diag(w_rg_x)], axis=2)
    return (w_in_r, wq_r.astype(bf16), wkv_r.astype(bf16), wgate.astype(bf16))


@functools.lru_cache(maxsize=None)
def _rope_tables(seq):
    pos = np.arange(seq, dtype=np.float64)
    inv_freq = ROPE_THETA ** (-np.arange(0, QK_ROPE_DIM, 2, dtype=np.float64) / QK_ROPE_DIM)
    ang = pos[:, None] * inv_freq[None, :]
    cos = np.tile(np.cos(ang), (1, 2))
    sin = np.tile(np.sin(ang), (1, 2))
    scale = math.log2(math.e) / math.sqrt(QK_HEAD_DIM)
    cq = np.tile(cos, (1, MLA_HEADS)) * scale
    sq = np.tile(sin, (1, MLA_HEADS)) * scale
    pads = ((0, 0), (QK_NOPE_DIM, HEAD_PAD - QK_HEAD_DIM))
    return tuple(t.astype(np.float32) for t in (cq, sq, np.pad(cos, pads), np.pad(sin, pads)))


def _const_spec(shape):
    return pl.BlockSpec(shape, lambda *_: (0,) * len(shape))


def kernel(x, norm_in_g, w_in, q_norm_g, w_q_b, kv_norm_g, w_kv_b, conv_w, conv_b,
           w_rg_a, b_rg_a, w_rg_x, b_rg_x, lru_param, out_norm_mla_g, out_norm_rnn_g,
           w_out, final_norm_g):
    b, s, _ = x.shape
    ts = SEQ_TILE
    tq, tk = Q_BLOCK, K_BLOCK
    assert s % ts == 0 and ts % tq == 0 and tq % tk == 0 and tk % CHUNK == 0
    assert s % OUT_TILE == 0
    assert w_in.shape[0] == 1, "the final norm is fused into the single layer's output kernel"
    cq, sq, ck, sk = _rope_tables(s)
    row = lambda v: v.reshape(1, -1).astype(f32)
    params = pltpu.CompilerParams(
        dimension_semantics=("arbitrary", "arbitrary"), vmem_limit_bytes=VMEM_LIMIT)

    w_in_r, wq, wkv, wgate = _prepare_weights(w_in[0], w_q_b[0], w_kv_b[0], w_rg_a[0], w_rg_x[0])
    bgate = jnp.stack([b_rg_a[0], b_rg_x[0]]).astype(f32)

    seq_spec = lambda w: pl.BlockSpec((1, ts, w), lambda bi, si: (bi, si, 0))
    tab_spec = lambda w: pl.BlockSpec((ts, w), lambda bi, si: (si, 0))
    qt, k, vt, gm, yr = pl.pallas_call(
        _proj_kernel,
        grid=(b, s // ts),
        in_specs=[
            seq_spec(D_MODEL), _const_spec((1, D_MODEL)), _const_spec((D_MODEL, Z_WIDTH)),
            _const_spec((1, Q_LORA_RANK)), _const_spec(wq.shape),
            _const_spec((1, KV_LORA_RANK)), _const_spec(wkv.shape),
            tab_spec(Q_PE_WIDTH), tab_spec(Q_PE_WIDTH), tab_spec(HEAD_PAD), tab_spec(HEAD_PAD),
            _const_spec((CONV_WIDTH, RNN_WIDTH)), _const_spec((1, RNN_WIDTH)),
            _const_spec(wgate.shape), _const_spec((2, RNN_WIDTH)),
            _const_spec((1, RNN_WIDTH)), _const_spec((1, RNN_WIDTH)),
        ],
        out_specs=[
            pl.BlockSpec((1, ts // tq, MLA_HEADS, HEAD_PAD, tq), lambda bi, si: (bi, si, 0, 0, 0)),
            pl.BlockSpec((1, MLA_HEADS, ts, HEAD_PAD), lambda bi, si: (bi, 0, si, 0)),
            pl.BlockSpec((1, ts // tk, MLA_HEADS, V_EXT, tk), lambda bi, si: (bi, si, 0, 0, 0)),
            seq_spec(MLA_WIDTH), seq_spec(RNN_WIDTH),
        ],
        out_shape=[
            jax.ShapeDtypeStruct((b, s // tq, MLA_HEADS, HEAD_PAD, tq), bf16),
            jax.ShapeDtypeStruct((b, MLA_HEADS, s, HEAD_PAD), bf16),
            jax.ShapeDtypeStruct((b, s // tk, MLA_HEADS, V_EXT, tk), bf16),
            jax.ShapeDtypeStruct((b, s, MLA_WIDTH), bf16),
            jax.ShapeDtypeStruct((b, s, RNN_WIDTH), bf16),
        ],
        scratch_shapes=[pltpu.VMEM((CONV_WIDTH - 1, SUBLANES, RNN_WIDTH), f32),
                        pltpu.VMEM((1, RNN_WIDTH), f32),
                        pltpu.VMEM((RNN_WIDTH // LANES, SUBLANES * PERM_PITCH, LANES), f32)],
        compiler_params=params,
        name="proj_rnn",
    )(x, row(norm_in_g[0]), w_in_r, row(q_norm_g[0]), wq, row(kv_norm_g[0]), wkv,
      cq, sq, ck, sk, conv_w[0].astype(f32), row(conv_b[0]), wgate, bgate,
      row(lru_param[0]), row(out_norm_rnn_g[0]))

    ym = pl.pallas_call(
        _attn_kernel,
        grid=(b, s // tq),
        in_specs=[
            pl.BlockSpec((1, 1, MLA_HEADS, HEAD_PAD, tq), lambda bi, qi: (bi, qi, 0, 0, 0)),
            pl.BlockSpec((1, MLA_HEADS, s, HEAD_PAD), lambda bi, qi: (bi, 0, 0, 0)),
            pl.BlockSpec((1, s // tk, MLA_HEADS, V_EXT, tk), lambda bi, qi: (bi, 0, 0, 0, 0)),
            pl.BlockSpec((1, tq, MLA_WIDTH), lambda bi, qi: (bi, qi, 0)),
            _const_spec((1, MLA_WIDTH)),
        ],
        out_specs=pl.BlockSpec((1, tq, MLA_WIDTH), lambda bi, qi: (bi, qi, 0)),
        out_shape=jax.ShapeDtypeStruct((b, s, MLA_WIDTH), bf16),
        scratch_shapes=[pltpu.VMEM((MLA_HEADS, tk, tq), f32),
                        pltpu.VMEM((MLA_HEADS, tk, tq), f32),
                        pltpu.VMEM((MLA_HEADS, 1, tq), f32),
                        pltpu.VMEM((MLA_HEADS, 1, tq), f32),
                        pltpu.VMEM((MLA_HEADS, 1, tq), f32),
                        pltpu.VMEM((MLA_HEADS, V_EXT, tq), f32)],
        compiler_params=params,
        name="mla_attention",
    )(qt, k, vt, gm, row(out_norm_mla_g[0]))

    to = OUT_TILE
    out_spec = lambda w: pl.BlockSpec((1, to, w), lambda bi, si: (bi, si, 0))
    return pl.pallas_call(
        _out_kernel,
        grid=(b, s // to),
        in_specs=[
            out_spec(MLA_WIDTH), out_spec(RNN_WIDTH), out_spec(D_MODEL),
            _const_spec((D_MODEL, D_MODEL)), _const_spec((1, D_MODEL)),
        ],
        out_specs=out_spec(D_MODEL),
        out_shape=jax.ShapeDtypeStruct((b, s, D_MODEL), f32),
        compiler_params=params,
        name="out_proj",
    )(ym, yr, x, w_out[0].astype(bf16), row(final_norm_g))
```

```python
import functools
import math

import jax
import jax.numpy as jnp
import numpy as np
from jax import lax
from jax.experimental import pallas as pl
from jax.experimental.pallas import tpu as pltpu

D_MODEL = 1024
CHUNK = 64
MLA_HEADS = 8
QK_NOPE_DIM = 64
QK_ROPE_DIM = 32
QK_HEAD_DIM = QK_NOPE_DIM + QK_ROPE_DIM
V_HEAD_DIM = 64
MLA_WIDTH = MLA_HEADS * V_HEAD_DIM
Q_LORA_RANK = 256
KV_LORA_RANK = 128
ROPE_THETA = 10000.0
RNN_WIDTH = 512
RNN_BLOCKS = 8
RNN_BLOCK_DIM = RNN_WIDTH // RNN_BLOCKS
CONV_WIDTH = 4
LRU_C = 8.0
NORM_EPS = 1e-6

LANES = 128
SUBLANES = 8
MXU_DIM = 256
HEAD_PAD = LANES
V_EXT = V_HEAD_DIM + 16
Z_WIDTH = 2048
SEQ_TILE = 512
OUT_TILE = 1024
K_BLOCK = MXU_DIM
Q_BLOCK = 2 * MXU_DIM
VMEM_LIMIT = 56 * 1024 * 1024

Z_Q, Z_KV, Z_KPE, Z_GM, Z_XR, Z_GR = 0, 256, 384, 512, 1024, 1536
Q_NOPE, Q_PE, Q_ROT = 0, MLA_HEADS * QK_NOPE_DIM, MLA_HEADS * (QK_NOPE_DIM + QK_ROPE_DIM)
Q_PE_WIDTH = MLA_HEADS * QK_ROPE_DIM
GATE_GROUP = MXU_DIM
PERM_PITCH = SEQ_TILE // SUBLANES + SUBLANES

MASK_VALUE = -0.5 * float(jnp.finfo(jnp.float32).max)

f32 = jnp.float32
bf16 = jnp.bfloat16


def _rms(x, g):
    ms = jnp.mean(x * x, axis=-1, keepdims=True)
    return x * lax.rsqrt(ms + NORM_EPS) * g


def _sigmoid(x):
    return 0.5 * jnp.tanh(0.5 * x) + 0.5


def _silu(x):
    hx = 0.5 * x
    return hx * jnp.tanh(hx) + hx


def _to_lockstep(x, buf):
    rows = x.shape[0]
    seg = rows // SUBLANES
    slabs = x.shape[1] // LANES
    for c in range(slabs):
        for r in range(SUBLANES):
            buf[c, r * PERM_PITCH:r * PERM_PITCH + seg, :] = (
                x[r * seg:(r + 1) * seg, c * LANES:(c + 1) * LANES])
    return [jnp.concatenate([buf[c, pl.ds(i, SUBLANES, stride=PERM_PITCH), :]
                             for c in range(slabs)], axis=1) for i in range(seg)]


def _from_lockstep(pieces, buf):
    seg = len(pieces)
    slabs = pieces[0].shape[1] // LANES
    for i, piece in enumerate(pieces):
        for c in range(slabs):
            buf[c, pl.ds(i, SUBLANES, stride=PERM_PITCH), :] = piece[:, c * LANES:(c + 1) * LANES]
    return jnp.concatenate(
        [jnp.concatenate([buf[c, r * PERM_PITCH:r * PERM_PITCH + seg, :] for c in range(slabs)],
                         axis=1) for r in range(SUBLANES)], axis=0)


def _proj_kernel(x_ref, gin_ref, win_ref, qg_ref, wq_ref, kvg_ref, wkv_ref,
                 cq_ref, sq_ref, ck_ref, sk_ref, convw_ref, convb_ref, wgate_ref,
                 bgate_ref, lru_ref, grnn_ref,
                 qt_out, k_out, vt_out, gm_out, yr_out,
                 xr_tail, h_carry, perm_buf):
    s = pl.program_id(1)
    ts = x_ref.shape[1]

    @pl.when(s == 0)
    def _():
        xr_tail[...] = jnp.zeros(xr_tail.shape, f32)
        h_carry[...] = jnp.zeros(h_carry.shape, f32)

    h = _rms(x_ref[0], gin_ref[...]).astype(bf16)

    def in_proj(lo, width):
        return jnp.dot(h, win_ref[:, lo:lo + width], preferred_element_type=f32)

    z_xr = in_proj(Z_XR, RNN_WIDTH)
    z_a = in_proj(0, Z_GM)
    xb, xr = _conv_lockstep(z_xr, xr_tail, perm_buf, convw_ref, convb_ref)
    gates = [jnp.dot(xb[:, g * GATE_GROUP:(g + 1) * GATE_GROUP], wgate_ref[g],
                     preferred_element_type=f32)
             for g in range(RNN_WIDTH // GATE_GROUP)]
    z_gm = in_proj(Z_GM, MLA_WIDTH)
    qn = _rms(z_a[:, Z_Q:Z_Q + Q_LORA_RANK], qg_ref[...]).astype(bf16)
    q = jnp.dot(qn, wq_ref[...], preferred_element_type=f32)
    kvn = _rms(z_a[:, Z_KV:Z_KV + KV_LORA_RANK], kvg_ref[...]).astype(bf16)
    kv = jnp.dot(kvn, wkv_ref[...], preferred_element_type=f32)
    z_gr = in_proj(Z_GR, RNN_WIDTH)

    y = _rg_lru(s, gates, xr, bgate_ref, lru_ref, h_carry, perm_buf)
    yr_out[0] = (_rms(y, grnn_ref[...]) * _silu(z_gr)).astype(bf16)

    scale = math.log2(math.e) / math.sqrt(QK_HEAD_DIM)
    q_nope_t = (q[:, Q_NOPE:Q_PE] * scale).T
    q_pe = q[:, Q_PE:Q_ROT] * cq_ref[...] + q[:, Q_ROT:] * sq_ref[...]
    q_pe_t = q_pe.T
    pad = jnp.zeros((HEAD_PAD - QK_HEAD_DIM, Q_BLOCK), bf16)
    for j in range(ts // Q_BLOCK):
        cols = slice(j * Q_BLOCK, (j + 1) * Q_BLOCK)
        for hh in range(MLA_HEADS):
            qt_out[0, j, hh, :QK_NOPE_DIM, :] = (
                q_nope_t[hh * QK_NOPE_DIM:(hh + 1) * QK_NOPE_DIM, cols].astype(bf16))
            qt_out[0, j, hh, QK_NOPE_DIM:QK_HEAD_DIM, :] = (
                q_pe_t[hh * QK_ROPE_DIM:(hh + 1) * QK_ROPE_DIM, cols].astype(bf16))
            qt_out[0, j, hh, QK_HEAD_DIM:, :] = pad

    kblk = z_a[:, Z_KPE:Z_KPE + HEAD_PAD]
    kpe = kblk * ck_ref[...] + pltpu.roll(kblk, LANES - QK_ROPE_DIM, 1) * sk_ref[...]
    for hh in range(MLA_HEADS):
        k_out[0, hh] = (kv[:, hh * HEAD_PAD:(hh + 1) * HEAD_PAD] + kpe).astype(bf16)
    vt = kv[:, MLA_HEADS * HEAD_PAD:].T
    ones_row = lax.broadcasted_iota(jnp.int32, (MLA_HEADS, V_EXT - V_HEAD_DIM, K_BLOCK), 1) == 0
    for j in range(ts // K_BLOCK):
        blk = vt[:, j * K_BLOCK:(j + 1) * K_BLOCK]
        vt_out[0, j, :, :V_HEAD_DIM, :] = (
            blk.reshape(MLA_HEADS, V_HEAD_DIM, K_BLOCK).astype(bf16))
        vt_out[0, j, :, V_HEAD_DIM:, :] = jnp.where(ones_row, 1.0, 0.0).astype(bf16)

    gm_out[0] = _silu(z_gm).astype(bf16)


def _conv_lockstep(x_rnn, xr_tail, perm_buf, convw_ref, convb_ref):
    ts = x_rnn.shape[0]
    sub8 = lax.broadcasted_iota(jnp.int32, (SUBLANES, RNN_WIDTH), 0)
    xl = _to_lockstep(x_rnn, perm_buf)
    hist = []
    for k in range(CONV_WIDTH - 1):
        prev = xr_tail[k]
        cur = xl[len(xl) - (CONV_WIDTH - 1) + k]
        hist.append(pltpu.roll(jnp.where(sub8 == SUBLANES - 1, prev, cur), 1, 0))
        xr_tail[k] = cur
    xext = jnp.concatenate(hist + xl, axis=0)
    xr = convb_ref[...]
    for k in range(CONV_WIDTH):
        xr = xr + convw_ref[k:k + 1, :] * xext[k * SUBLANES:k * SUBLANES + ts]
    return xr.astype(bf16), xr


def _rg_lru(s, gates, xr, bgate_ref, lru_ref, h_carry, perm_buf):
    ts = xr.shape[0]
    sub8 = lax.broadcasted_iota(jnp.int32, (SUBLANES, RNN_WIDTH), 0)
    ga = [gg[:, :GATE_GROUP] for gg in gates]
    gx = [gg[:, GATE_GROUP:] for gg in gates]
    r = _sigmoid(jnp.concatenate(ga, axis=1) + bgate_ref[0:1, :])
    i = _sigmoid(jnp.concatenate(gx, axis=1) + bgate_ref[1:2, :])
    nl = -lru_ref[...]
    softplus = jnp.maximum(nl, 0.0) + jnp.log1p(jnp.exp(-jnp.abs(nl)))
    a = jnp.exp2(r * ((-LRU_C * math.log2(math.e)) * softplus))
    m2 = jnp.maximum(1.0 - a * a, 1e-12)
    mult = m2 * lax.rsqrt(m2)
    row = lax.broadcasted_iota(jnp.int32, (ts, RNN_WIDTH), 0)
    mult = jnp.where((row == 0) & (s == 0), 1.0, mult)
    u = mult * (i * xr)

    steps = ts // SUBLANES
    hl, pr = [], []
    hcur = pcur = None
    for t in range(steps):
        at = a[t * SUBLANES:(t + 1) * SUBLANES]
        ut = u[t * SUBLANES:(t + 1) * SUBLANES]
        hcur = ut if t == 0 else at * hcur + ut
        pcur = at if t == 0 else at * pcur
        hl.append(hcur)
        pr.append(pcur)
    pe, he = pcur, hcur
    for d in (1, 2, 4):
        keep = sub8 >= d
        he = jnp.where(keep, pe * pltpu.roll(he, d, 0) + he, he)
        pe = jnp.where(keep, pe * pltpu.roll(pe, d, 0), pe)
    carry = h_carry[...]
    ends = he + pe * carry
    h_carry[...] = ends[SUBLANES - 1:SUBLANES, :]
    init = jnp.where(sub8 == 0, carry, pltpu.roll(ends, 1, 0))
    return _from_lockstep([hl[t] + pr[t] * init for t in range(steps)], perm_buf)


def _attn_kernel(qt_ref, k_ref, vt_ref, gm_ref, gmla_ref, o_ref,
                 s_even, s_odd, c_even, c_odd, m_ref, acc_ref):
    qi = pl.program_id(1)
    tk = K_BLOCK
    per_q = Q_BLOCK // K_BLOCK
    first_own = per_q * qi
    even = (s_even, c_even)
    odd = (s_odd, c_odd)

    m_ref[...] = jnp.full(m_ref.shape, MASK_VALUE, f32)
    acc_ref[...] = jnp.zeros(acc_ref.shape, f32)

    def qk(hh, j, dst, own):
        s_dst, c_dst = dst
        cols = visible(own)
        kb = k_ref[0, hh, pl.ds(pl.multiple_of(j * tk, tk), tk), :]
        st = jnp.dot(kb, qt_ref[0, 0, hh, :, cols], preferred_element_type=f32)
        if own is not None:
            kc = (lax.broadcasted_iota(jnp.int32, st.shape, 0) + own * tk) // CHUNK
            qc = (lax.broadcasted_iota(jnp.int32, st.shape, 1) + own * tk) // CHUNK
            st = jnp.where(kc <= qc, st, MASK_VALUE)
        s_dst[hh, :, cols] = st
        c_dst[hh, :, cols] = jnp.max(st, axis=0, keepdims=True)

    def visible(own):
        return slice(0 if own is None else own * tk, Q_BLOCK)

    def attend(hh, j, src, own=None):
        s_src, c_src = src
        cols = visible(own)
        m_prev = m_ref[hh, :, cols]
        m_new = jnp.maximum(m_prev, c_src[hh, :, cols])
        alpha = jnp.exp2(m_prev - m_new)
        p = jnp.exp2(s_src[hh, :, cols] - m_new).astype(bf16)
        upd = jnp.dot(vt_ref[0, j, hh], p, preferred_element_type=f32)
        acc_ref[hh, :, cols] = alpha * acc_ref[hh, :, cols] + upd
        m_ref[hh, :, cols] = m_new

    def step(j, cur, nxt, next_own, cur_own=None):
        for hh in range(MLA_HEADS):
            qk(hh, j + 1, nxt, next_own)
            attend(hh, j, cur, cur_own)

    def finish(j, cur):
        heads = []
        for hh in range(MLA_HEADS):
            attend(hh, j, cur, per_q - 1)
            acc = acc_ref[hh]
            heads.append(acc[:V_HEAD_DIM] / acc[V_HEAD_DIM:V_HEAD_DIM + 1])
        y = jnp.concatenate(heads, axis=0).T
        o_ref[0] = (_rms(y, gmla_ref[...]) * gm_ref[0].astype(f32)).astype(bf16)

    @pl.when(qi == 0)
    def _():
        for hh in range(MLA_HEADS):
            qk(hh, 0, even, 0)
        step(0, even, odd, 1)
        finish(1, odd)

    @pl.when(qi > 0)
    def _():
        for hh in range(MLA_HEADS):
            qk(hh, 0, even, None)

    def quad(jj, carry):
        for t in range(0, 4, 2):
            step(4 * jj + t, even, odd, None)
            step(4 * jj + t + 1, odd, even, None)
        return carry

    lax.fori_loop(0, lax.shift_right_logical(jnp.maximum(qi - 1, 0), 1), quad, 0)

    def close():
        step(first_own - 2, even, odd, None)
        step(first_own - 1, odd, even, 0)
        step(first_own, even, odd, 1)
        finish(first_own + 1, odd)

    @pl.when((qi > 0) & (lax.rem(qi, 2) == 1))
    def _():
        close()

    @pl.when((qi > 0) & (lax.rem(qi, 2) == 0))
    def _():
        step(first_own - 4, even, odd, None)
        step(first_own - 3, odd, even, None)
        close()


def _out_kernel(ym_ref, yr_ref, x_ref, wout_ref, gfin_ref, o_ref):
    y = jnp.concatenate([ym_ref[0], yr_ref[0]], axis=-1)
    xo = x_ref[0] + jnp.dot(y, wout_ref[...], preferred_element_type=f32)
    o_ref[0] = _rms(xo, gfin_ref[...])


def _rot_half(w):
    half = QK_ROPE_DIM // 2
    return jnp.concatenate([-w[..., half:], w[..., :half]], axis=-1)


def _win_layout_kernel(wt_ref, o_ref):
    wt = wt_ref[...]
    lat = Q_LORA_RANK + KV_LORA_RANK
    rest = lat + QK_ROPE_DIM
    half = QK_ROPE_DIM // 2
    kr = wt[lat:rest]
    rows = jnp.concatenate(
        [wt[:lat], jnp.zeros((QK_NOPE_DIM, wt.shape[1]), f32), kr, -kr[half:], kr[:half],
         wt[rest:]], axis=0)
    o_ref[...] = rows.T.astype(bf16)


def _prepare_weights(w_in, w_q_b, w_kv_b, w_rg_a, w_rg_x):
    kb = 256
    w_in_r = pl.pallas_call(
        _win_layout_kernel,
        grid=(D_MODEL // kb,),
        in_specs=[pl.BlockSpec((w_in.shape[1], kb), lambda i: (0, i))],
        out_specs=pl.BlockSpec((kb, Z_WIDTH), lambda i: (i, 0)),
        out_shape=jax.ShapeDtypeStruct((D_MODEL, Z_WIDTH), bf16),
        name="w_in_layout",
    )(w_in.T)

    wq = w_q_b.reshape(Q_LORA_RANK, MLA_HEADS, QK_HEAD_DIM)
    wq_nope = wq[:, :, :QK_NOPE_DIM].reshape(Q_LORA_RANK, -1)
    wq_pe = wq[:, :, QK_NOPE_DIM:]
    wq_r = jnp.concatenate([wq_nope, wq_pe.reshape(Q_LORA_RANK, -1),
                            _rot_half(wq_pe).reshape(Q_LORA_RANK, -1)], axis=1)

    wkv = w_kv_b.reshape(KV_LORA_RANK, MLA_HEADS, QK_NOPE_DIM + V_HEAD_DIM)
    wk = jnp.pad(wkv[:, :, :QK_NOPE_DIM], ((0, 0), (0, 0), (0, HEAD_PAD - QK_NOPE_DIM)))
    wk = wk.reshape(KV_LORA_RANK, MLA_HEADS * HEAD_PAD)
    wv = wkv[:, :, QK_NOPE_DIM:].reshape(KV_LORA_RANK, MLA_WIDTH)
    wkv_r = jnp.concatenate([wk, wv], axis=1)

    per = GATE_GROUP // RNN_BLOCK_DIM
    eye = jnp.eye(per, dtype=w_rg_a.dtype)

    def block_diag(w):
        w = w.reshape(RNN_BLOCKS // per, per, RNN_BLOCK_DIM, RNN_BLOCK_DIM)
        return jnp.einsum('gncd,nm->gncmd', w, eye).reshape(-1, GATE_GROUP, GATE_GROUP)

    wgate = jnp.concatenate([block_diag(w_rg_a), block_diag(w_rg_x)], axis=2)
    return (w_in_r, wq_r.astype(bf16), wkv_r.astype(bf16), wgate.astype(bf16))


@functools.lru_cache(maxsize=None)
def _rope_tables(seq):
    pos = np.arange(seq, dtype=np.float64)
    inv_freq = ROPE_THETA ** (-np.arange(0, QK_ROPE_DIM, 2, dtype=np.float64) / QK_ROPE_DIM)
    ang = pos[:, None] * inv_freq[None, :]
    cos = np.tile(np.cos(ang), (1, 2))
    sin = np.tile(np.sin(ang), (1, 2))
    scale = math.log2(math.e) / math.sqrt(QK_HEAD_DIM)
    cq = np.tile(cos, (1, MLA_HEADS)) * scale
    sq = np.tile(sin, (1, MLA_HEADS)) * scale
    pads = ((0, 0), (QK_NOPE_DIM, HEAD_PAD - QK_HEAD_DIM))
    return tuple(t.astype(np.float32) for t in (cq, sq, np.pad(cos, pads), np.pad(sin, pads)))


def _const_spec(shape):
    return pl.BlockSpec(shape, lambda *_: (0,) * len(shape))


def kernel(x, norm_in_g, w_in, q_norm_g, w_q_b, kv_norm_g, w_kv_b, conv_w, conv_b,
           w_rg_a, b_rg_a, w_rg_x, b_rg_x, lru_param, out_norm_mla_g, out_norm_rnn_g,
           w_out, final_norm_g):
    b, s, _ = x.shape
    ts = SEQ_TILE
    tq, tk = Q_BLOCK, K_BLOCK
    assert s % ts == 0 and ts % tq == 0 and tq % tk == 0 and tk % CHUNK == 0
    assert s % OUT_TILE == 0
    assert w_in.shape[0] == 1, "the final norm is fused into the single layer's output kernel"
    cq, sq, ck, sk = _rope_tables(s)
    row = lambda v: v.reshape(1, -1).astype(f32)
    params = pltpu.CompilerParams(
        dimension_semantics=("arbitrary", "arbitrary"), vmem_limit_bytes=VMEM_LIMIT)

    w_in_r, wq, wkv, wgate = _prepare_weights(w_in[0], w_q_b[0], w_kv_b[0], w_rg_a[0], w_rg_x[0])
    bgate = jnp.stack([b_rg_a[0], b_rg_x[0]]).astype(f32)

    seq_spec = lambda w: pl.BlockSpec((1, ts, w), lambda bi, si: (bi, si, 0))
    tab_spec = lambda w: pl.BlockSpec((ts, w), lambda bi, si: (si, 0))
    qt, k, vt, gm, yr = pl.pallas_call(
        _proj_kernel,
        grid=(b, s // ts),
        in_specs=[
            seq_spec(D_MODEL), _const_spec((1, D_MODEL)), _const_spec((D_MODEL, Z_WIDTH)),
            _const_spec((1, Q_LORA_RANK)), _const_spec(wq.shape),
            _const_spec((1, KV_LORA_RANK)), _const_spec(wkv.shape),
            tab_spec(Q_PE_WIDTH), tab_spec(Q_PE_WIDTH), tab_spec(HEAD_PAD), tab_spec(HEAD_PAD),
            _const_spec((CONV_WIDTH, RNN_WIDTH)), _const_spec((1, RNN_WIDTH)),
            _const_spec(wgate.shape), _const_spec((2, RNN_WIDTH)),
            _const_spec((1, RNN_WIDTH)), _const_spec((1, RNN_WIDTH)),
        ],
        out_specs=[
            pl.BlockSpec((1, ts // tq, MLA_HEADS, HEAD_PAD, tq), lambda bi, si: (bi, si, 0, 0, 0)),
            pl.BlockSpec((1, MLA_HEADS, ts, HEAD_PAD), lambda bi, si: (bi, 0, si, 0)),
            pl.BlockSpec((1, ts // tk, MLA_HEADS, V_EXT, tk), lambda bi, si: (bi, si, 0, 0, 0)),
            seq_spec(MLA_WIDTH), seq_spec(RNN_WIDTH),
        ],
        out_shape=[
            jax.ShapeDtypeStruct((b, s // tq, MLA_HEADS, HEAD_PAD, tq), bf16),
            jax.ShapeDtypeStruct((b, MLA_HEADS, s, HEAD_PAD), bf16),
            jax.ShapeDtypeStruct((b, s // tk, MLA_HEADS, V_EXT, tk), bf16),
            jax.ShapeDtypeStruct((b, s, MLA_WIDTH), bf16),
            jax.ShapeDtypeStruct((b, s, RNN_WIDTH), bf16),
        ],
        scratch_shapes=[pltpu.VMEM((CONV_WIDTH - 1, SUBLANES, RNN_WIDTH), f32),
                        pltpu.VMEM((1, RNN_WIDTH), f32),
                        pltpu.VMEM((RNN_WIDTH // LANES, SUBLANES * PERM_PITCH, LANES), f32)],
        compiler_params=params,
        name="proj_rnn",
    )(x, row(norm_in_g[0]), w_in_r, row(q_norm_g[0]), wq, row(kv_norm_g[0]), wkv,
      cq, sq, ck, sk, conv_w[0].astype(f32), row(conv_b[0]), wgate, bgate,
      row(lru_param[0]), row(out_norm_rnn_g[0]))

    ym = pl.pallas_call(
        _attn_kernel,
        grid=(b, s // tq),
        in_specs=[
            pl.BlockSpec((1, 1, MLA_HEADS, HEAD_PAD, tq), lambda bi, qi: (bi, qi, 0, 0, 0)),
            pl.BlockSpec((1, MLA_HEADS, s, HEAD_PAD), lambda bi, qi: (bi, 0, 0, 0)),
            pl.BlockSpec((1, s // tk, MLA_HEADS, V_EXT, tk), lambda bi, qi: (bi, 0, 0, 0, 0)),
            pl.BlockSpec((1, tq, MLA_WIDTH), lambda bi, qi: (bi, qi, 0)),
            _const_spec((1, MLA_WIDTH)),
        ],
        out_specs=pl.BlockSpec((1, tq, MLA_WIDTH), lambda bi, qi: (bi, qi, 0)),
        out_shape=jax.ShapeDtypeStruct((b, s, MLA_WIDTH), bf16),
        scratch_shapes=[pltpu.VMEM((MLA_HEADS, tk, tq), f32),
                        pltpu.VMEM((MLA_HEADS, tk, tq), f32),
                        pltpu.VMEM((MLA_HEADS, 1, tq), f32),
                        pltpu.VMEM((MLA_HEADS, 1, tq), f32),
                        pltpu.VMEM((MLA_HEADS, 1, tq), f32),
                        pltpu.VMEM((MLA_HEADS, V_EXT, tq), f32)],
        compiler_params=params,
        name="mla_attention",
    )(qt, k, vt, gm, row(out_norm_mla_g[0]))

    to = OUT_TILE
    out_spec = lambda w: pl.BlockSpec((1, to, w), lambda bi, si: (bi, si, 0))
    return pl.pallas_call(
        _out_kernel,
        grid=(b, s // to),
        in_specs=[
            out_spec(MLA_WIDTH), out_spec(RNN_WIDTH), out_spec(D_MODEL),
            _const_spec((D_MODEL, D_MODEL)), _const_spec((1, D_MODEL)),
        ],
        out_specs=out_spec(D_MODEL),
        out_shape=jax.ShapeDtypeStruct((b, s, D_MODEL), f32),
        compiler_params=params,
        name="out_proj",
    )(ym, yr, x, w_out[0].astype(bf16), row(final_norm_g))
```
